```python
import jax, jax.numpy as jnp
from jax import lax
import numpy as np

D_MODEL = 1024
BATCH = 32
SEQ = 2048
DEPTH = 1

CHUNK = 128
GM_HEAD_DIM = 128
GM_WIDTH = D_MODEL // 2
GM_HEADS = GM_WIDTH // GM_HEAD_DIM
ML_HEAD_DIM = 128
ML_WIDTH = D_MODEL - GM_WIDTH
ML_HEADS = ML_WIDTH // ML_HEAD_DIM
D_MIX = GM_WIDTH + ML_WIDTH
ML_CONV = 4
FFN_CONV = 3
D_FF = 2816
EPS = 1e-6
IN_SPLITS = [GM_WIDTH, GM_WIDTH, ML_WIDTH, ML_WIDTH, ML_WIDTH, ML_WIDTH, ML_HEADS, ML_HEADS]
N_IN = sum(IN_SPLITS)

kernel_name = 'hybrid_gmlp_mlstm_convffn_adaln'


def rmsnorm(x, g):
    xf = x.astype(jnp.float32)
    y = xf * lax.rsqrt(jnp.mean(xf * xf, axis=-1, keepdims=True) + EPS)
    return (y * g.astype(jnp.float32)).astype(x.dtype)


def modulate(h, shift, scale):
    return h * (1 + scale[:, None, :]) + shift[:, None, :]


def causal_dwconv(x, w, b):
    K, C = w.shape
    y = lax.conv_general_dilated(
        x, w[:, None, :].astype(x.dtype), window_strides=(1,), padding=[(K - 1, 0)],
        dimension_numbers=('NWC', 'WIO', 'NWC'), feature_group_count=C)
    return y + b.astype(x.dtype)


def gmlp_mix(u, v, vnorm_g, w_s, b_s):
    B, S, _ = u.shape
    nc = S // CHUNK
    u = jax.nn.gelu(u, approximate=False)
    v = jax.nn.gelu(v, approximate=False)
    v = rmsnorm(v.reshape(B, S, GM_HEADS, GM_HEAD_DIM), vnorm_g.reshape(GM_HEADS, GM_HEAD_DIM))
    v = v.reshape(B, nc, CHUNK, GM_HEADS, GM_HEAD_DIM)
    mask = jnp.tril(jnp.ones((CHUNK, CHUNK), dtype=bool))
    w = jnp.where(mask[None], w_s, jnp.zeros_like(w_s))
    mixed = jnp.einsum('hts,bcshe->bcthe', w, v) + jnp.transpose(b_s)[None, None, :, :, None]
    return u * mixed.reshape(B, S, GM_WIDTH)


def mlstm_cell(q, k, v, i_pre, f_pre):
    B, S, H, dh = q.shape
    nc = S // CHUNK
    to_chunks = lambda t: jnp.transpose(t.reshape(B, nc, CHUNK, H, dh), (0, 3, 1, 2, 4))
    gate_chunks = lambda t: jnp.transpose(t.reshape(B, nc, CHUNK, H), (0, 3, 1, 2))
    q = to_chunks(q)
    k = to_chunks(k) * (dh ** -0.5)
    v = to_chunks(v)
    ii = gate_chunks(i_pre)
    logf = jax.nn.log_sigmoid(gate_chunks(f_pre))
    b = jnp.cumsum(logf, axis=-1)
    bL = b[..., -1]
    a = bL[..., None] - b + ii
    a_max = jnp.max(a, axis=-1)
    wgt = jnp.exp(a - a_max[..., None])
    S_c = jnp.einsum('bhcs,bhcse,bhcsd->bhced', wgt, v, k)
    n_c = jnp.einsum('bhcs,bhcsd->bhcd', wgt, k)

    def step(carry, inp):
        C, n, m = carry
        Sc, nc_, bl, am = inp
        m_new = jnp.maximum(bl + m, am)
        decay = jnp.exp(bl + m - m_new)
        inw = jnp.exp(am - m_new)
        C_new = decay[..., None, None] * C + inw[..., None, None] * Sc
        n_new = decay[..., None] * n + inw[..., None] * nc_
        return (C_new, n_new, m_new), (C, n, m)

    init = (jnp.zeros((B, H, dh, dh), jnp.float32), jnp.zeros((B, H, dh), jnp.float32),
            jnp.zeros((B, H), jnp.float32))
    xs = (jnp.moveaxis(S_c, 2, 0), jnp.moveaxis(n_c, 2, 0), jnp.moveaxis(bL, 2, 0), jnp.moveaxis(a_max, 2, 0))
    _, (C_prev, n_prev, m_prev) = lax.scan(step, init, xs)
    C_prev = jnp.moveaxis(C_prev, 0, 2)
    n_prev = jnp.moveaxis(n_prev, 0, 2)
    m_prev = jnp.moveaxis(m_prev, 0, 2)

    mask = jnp.tril(jnp.ones((CHUNK, CHUNK), dtype=bool))
    D = b[..., :, None] - b[..., None, :] + ii[..., None, :]
    D = jnp.where(mask, D, -jnp.inf)
    m_intra = jnp.max(D, axis=-1)
    inter = b + m_prev[..., None]
    m = jnp.maximum(inter, m_intra)
    P = jnp.exp(D - m[..., None])
    scores = jnp.einsum('bhcjd,bhcsd->bhcjs', q, k) * P
    inter_w = jnp.exp(inter - m)
    num = (jnp.einsum('bhcjs,bhcse->bhcje', scores, v)
           + inter_w[..., None] * jnp.einsum('bhcjd,bhced->bhcje', q, C_prev))
    den = jnp.sum(scores, axis=-1) + inter_w * jnp.einsum('bhcjd,bhcd->bhcj', q, n_prev)
    h = num / jnp.maximum(jnp.abs(den), jnp.exp(-m))[..., None]
    return jnp.transpose(h, (0, 2, 3, 1, 4)).reshape(B, S, H, dh)


def mlstm_mix(q_pre, k_pre, v, o, i_pre, f_pre, conv_w, conv_b, i_b, f_b, hnorm_g):
    B, S, _ = q_pre.shape
    qk = jax.nn.silu(causal_dwconv(jnp.concatenate([q_pre, k_pre], axis=-1), conv_w, conv_b))
    q, k = jnp.split(qk, 2, axis=-1)
    shp = (B, S, ML_HEADS, ML_HEAD_DIM)
    h = mlstm_cell(q.reshape(shp).astype(jnp.float32), k.reshape(shp).astype(jnp.float32),
                   v.reshape(shp).astype(jnp.float32),
                   (i_pre + i_b).astype(jnp.float32), (f_pre + f_b).astype(jnp.float32))
    h = rmsnorm(h, hnorm_g.reshape(ML_HEADS, ML_HEAD_DIM)).reshape(B, S, ML_WIDTH)
    return (jax.nn.sigmoid(o.astype(jnp.float32)) * h).astype(q_pre.dtype)


def conv_ffn(h, w_up, conv_w, conv_b, w_down):
    g, u = jnp.split(h @ w_up, 2, axis=-1)
    g = jax.nn.gelu(causal_dwconv(g, conv_w, conv_b), approximate=False)
    return (g * u) @ w_down


def setup_inputs(seed: int = 0) -> dict:
    key = jax.random.key(seed)
    ks = jax.random.split(key, 24)
    f32 = jnp.float32
    nrm = lambda k, shape, fan_in: jax.random.normal(k, shape, f32) * (fan_in ** -0.5)
    rnd = lambda k, shape: jax.random.normal(k, shape, f32)
    return {
        'x': rnd(ks[0], (BATCH, SEQ, D_MODEL)),
        'c': rnd(ks[1], (BATCH, D_MODEL)),
        'ada_w': 0.5 * nrm(ks[2], (DEPTH, D_MODEL, 6 * D_MODEL), D_MODEL),
        'ada_b': 0.02 * rnd(ks[3], (DEPTH, 6 * D_MODEL)),
        'norm1_g': 1.0 + 0.02 * rnd(ks[4], (DEPTH, D_MODEL)),
        'w_in': nrm(ks[5], (DEPTH, D_MODEL, N_IN), D_MODEL),
        'gm_vnorm_g': 1.0 + 0.02 * rnd(ks[6], (DEPTH, GM_WIDTH)),
        'gm_spatial_w': nrm(ks[7], (DEPTH, GM_HEADS, CHUNK, CHUNK), CHUNK),
        'gm_spatial_b': 1.0 + 0.1 * rnd(ks[8], (DEPTH, GM_HEADS, CHUNK)),
        'ml_conv_w': nrm(ks[9], (DEPTH, ML_CONV, 2 * ML_WIDTH), ML_CONV),
        'ml_conv_b': 0.02 * rnd(ks[10], (DEPTH, 2 * ML_WIDTH)),
        'ml_i_b': 0.1 * rnd(ks[11], (DEPTH, ML_HEADS)),
        'ml_f_b': jnp.linspace(3.0, 6.0, ML_HEADS, dtype=f32)[None, :] + 0.1 * rnd(ks[12], (DEPTH, ML_HEADS)),
        'ml_hnorm_g': 1.0 + 0.02 * rnd(ks[13], (DEPTH, ML_WIDTH)),
        'w_out': nrm(ks[14], (DEPTH, D_MIX, D_MODEL), D_MIX),
        'norm2_g': 1.0 + 0.02 * rnd(ks[15], (DEPTH, D_MODEL)),
        'ffn_w_up': nrm(ks[16], (DEPTH, D_MODEL, 2 * D_FF), D_MODEL),
        'ffn_conv_w': nrm(ks[17], (DEPTH, FFN_CONV, D_FF), FFN_CONV),
        'ffn_conv_b': 0.02 * rnd(ks[18], (DEPTH, D_FF)),
        'ffn_w_down': nrm(ks[19], (DEPTH, D_FF, D_MODEL), D_FF),
        'final_ada_w': 0.5 * nrm(ks[20], (D_MODEL, 2 * D_MODEL), D_MODEL),
        'final_ada_b': 0.02 * rnd(ks[21], (2 * D_MODEL,)),
        'final_g': 1.0 + 0.02 * rnd(ks[22], (D_MODEL,)),
    }


def reference(x, c, ada_w, ada_b, norm1_g, w_in, gm_vnorm_g, gm_spatial_w, gm_spatial_b,
              ml_conv_w, ml_conv_b, ml_i_b, ml_f_b, ml_hnorm_g, w_out, norm2_g,
              ffn_w_up, ffn_conv_w, ffn_conv_b, ffn_w_down, final_ada_w, final_ada_b, final_g):
    split_idx = np.cumsum(IN_SPLITS)[:-1].tolist()
    c_act = jax.nn.silu(c)
    for l in range(DEPTH):
        mod = c_act @ ada_w[l] + ada_b[l]
        sh1, sc1, g1, sh2, sc2, g2 = jnp.split(mod, 6, axis=-1)
        h = modulate(rmsnorm(x, norm1_g[l]), sh1, sc1)
        gu, gv, mq, mk, mv, mo, mi, mf = jnp.split(h @ w_in[l], split_idx, axis=-1)
        y_gm = gmlp_mix(gu, gv, gm_vnorm_g[l], gm_spatial_w[l], gm_spatial_b[l])
        y_ml = mlstm_mix(mq, mk, mv, mo, mi, mf, ml_conv_w[l], ml_conv_b[l],
                         ml_i_b[l], ml_f_b[l], ml_hnorm_g[l])
        y = jnp.concatenate([y_gm, y_ml], axis=-1) @ w_out[l]
        x = x + g1[:, None, :] * y
        h = modulate(rmsnorm(x, norm2_g[l]), sh2, sc2)
        x = x + g2[:, None, :] * conv_ffn(h, ffn_w_up[l], ffn_conv_w[l], ffn_conv_b[l], ffn_w_down[l])
    fmod = c_act @ final_ada_w + final_ada_b
    f_sh, f_sc = jnp.split(fmod, 2, axis=-1)
    return modulate(rmsnorm(x, final_g), f_sh, f_sc)
```

```python
import functools

import jax
import jax.numpy as jnp
from jax import lax
from jax.experimental import pallas as pl
from jax.experimental.pallas import tpu as pltpu

D_MODEL = 1024
CHUNK = 128
HEAD_DIM = 128
N_HEADS = 4
GM_WIDTH = N_HEADS * HEAD_DIM
ML_WIDTH = N_HEADS * HEAD_DIM
ML_CONV = 4
FFN_CONV = 3
D_FF = 2816
EPS = 1e-6
N_MAIN = 2 * GM_WIDTH + 4 * ML_WIDTH
SUBLANES = 8
VMEM_LIMIT_BYTES = 56 * 1024 * 1024

F32 = jnp.float32
BF16 = jnp.bfloat16


def _gelu(x):
    return 0.5 * x * (1.0 + lax.erf(x * 0.7071067811865476))


def _silu(x):
    return x * jax.nn.sigmoid(x)


def _log_sigmoid(x):
    return jnp.minimum(x, 0.0) - jnp.log1p(jnp.exp(-jnp.abs(x)))


def _rms_scale(x):
    return lax.rsqrt(jnp.mean(x * x, axis=-1, keepdims=True) + EPS)


def _causal_conv(cur, prev, w_ref, cols, bias, width):
    n = cur.shape[0]
    acc = cur * w_ref[width - 1:width, cols] + bias
    sub = lax.broadcasted_iota(jnp.int32, (SUBLANES, cur.shape[1]), 0)
    for d in range(1, width):
        rolled = pltpu.roll(cur, d, axis=0)
        head = jnp.where(sub < d, pltpu.roll(prev, d, axis=0), rolled[0:SUBLANES])
        shifted = jnp.concatenate([head, rolled[SUBLANES:n]], axis=0)
        acc = acc + shifted * w_ref[width - 1 - d:width - d, cols]
    return acc


def _mod_kernel(c_ref, w_ref, b_ref, o_ref):
    ca = _silu(c_ref[...]).astype(BF16)
    o_ref[...] = jnp.dot(ca, w_ref[...].astype(BF16), preferred_element_type=F32) + b_ref[...]


def _modulation(c, w_all, b_all):
    bsz = c.shape[0]
    n = w_all.shape[1]
    bn = D_MODEL
    return pl.pallas_call(
        _mod_kernel,
        grid=(n // bn,),
        in_specs=[
            pl.BlockSpec((bsz, D_MODEL), lambda j: (0, 0)),
            pl.BlockSpec((D_MODEL, bn), lambda j: (0, j)),
            pl.BlockSpec((1, bn), lambda j: (0, j)),
        ],
        out_specs=pl.BlockSpec((bsz, bn), lambda j: (0, j)),
        out_shape=jax.ShapeDtypeStruct((bsz, n), F32),
        compiler_params=pltpu.CompilerParams(dimension_semantics=("arbitrary",)),
        name="adaln_modulation",
    )(c, w_all, b_all)


def _mix_kernel(x_ref, mod_ref, n1g_ref, win_ref, wgt_ref, gb_ref, vng_ref, ws_ref, bs_ref,
                cw_ref, cb_ref, hng_ref, wout_ref, o_ref,
                hb_ref, proj_ref, ymix_ref, cprev_ref, mst_ref, m_ref, *, ts):
    nc = ts // CHUNK

    @pl.when(pl.program_id(1) == 0)
    def _():
        cprev_ref[...] = jnp.zeros_like(cprev_ref)
        mst_ref[...] = jnp.zeros_like(mst_ref)
        m_ref[...] = jnp.zeros_like(m_ref)

    mod = mod_ref[0]
    sh1, sc1, g1 = mod[0:1], mod[1:2], mod[2:3]

    for c in range(nc):
        rows = slice(c * CHUNK, (c + 1) * CHUNK)
        xs = x_ref[0, rows, :]
        y = xs * _rms_scale(xs) * n1g_ref[...]
        hb_ref[rows, :] = (y * (1.0 + sc1) + sh1).astype(BF16)

    proj_ref[...] = jnp.dot(hb_ref[...], win_ref[...], preferred_element_type=F32)
    gates = lax.dot_general(wgt_ref[...], hb_ref[...], (((1,), (1,)), ((), ())),
                            preferred_element_type=F32)
    gates = gates[0:SUBLANES] + gb_ref[:, 0:1]

    ii = gates
    logf = _log_sigmoid(pltpu.roll(gates, N_HEADS, axis=0))
    lane = lax.broadcasted_iota(jnp.int32, (SUBLANES, ts), 1) % CHUNK
    b = logf
    for d in (1, 2, 4, 8, 16, 32, 64):
        b = b + jnp.where(lane >= d, pltpu.roll(b, d, axis=1), 0.0)
    r = ii - b
    cr = r
    for d in (1, 2, 4, 8, 16, 32, 64):
        cr = jnp.maximum(cr, jnp.where(lane >= d, pltpu.roll(cr, d, axis=1), -jnp.inf))

    tri_r = lax.broadcasted_iota(jnp.int32, (CHUNK, CHUNK), 0)
    tri_c = lax.broadcasted_iota(jnp.int32, (CHUNK, CHUNK), 1)
    tril = tri_c <= tri_r
    ones_col = jnp.where(tri_c == 0, 1.0, 0.0).astype(BF16)
    ws = [jnp.where(tril, ws_ref[h], 0.0).astype(BF16) for h in range(N_HEADS)]
    zpad = jnp.zeros((CHUNK - 2 * SUBLANES, CHUNK), F32)

    m = m_ref[:, 0:1]
    for c in range(nc):
        rows = slice(c * CHUNK, (c + 1) * CHUNK)
        b_c, r_c, cr_c, ii_c = b[:, rows], r[:, rows], cr[:, rows], ii[:, rows]
        b_last = b_c[:, CHUNK - 1:CHUNK]
        a_c = b_last - b_c + ii_c
        a_max = jnp.max(a_c, axis=1, keepdims=True)
        wgt = jnp.exp(a_c - a_max)
        m_new = jnp.maximum(b_last + m, a_max)
        decay = jnp.exp(b_last + m - m_new)
        inw = jnp.exp(a_max - m_new)
        zt = jnp.concatenate([cr_c, b_c, zpad], axis=0).T

        for h in range(N_HEADS):
            hc = slice(h * HEAD_DIM, (h + 1) * HEAD_DIM)
            u = _gelu(proj_ref[rows, hc])
            v = _gelu(proj_ref[rows, GM_WIDTH + h * HEAD_DIM:GM_WIDTH + (h + 1) * HEAD_DIM])
            v = v * _rms_scale(v) * vng_ref[:, hc]
            mixed = jnp.dot(ws[h], v.astype(BF16), preferred_element_type=F32) + bs_ref[:, h:h + 1]
            ymix_ref[rows, hc] = (u * mixed).astype(BF16)

            qc = slice(h * HEAD_DIM, (h + 1) * HEAD_DIM)
            kc = slice(ML_WIDTH + h * HEAD_DIM, ML_WIDTH + (h + 1) * HEAD_DIM)
            base = 2 * GM_WIDTH
            q_pre = proj_ref[rows, base + qc.start:base + qc.stop]
            k_pre = proj_ref[rows, base + kc.start:base + kc.stop]
            q = _silu(_causal_conv(q_pre, cprev_ref[:, qc], cw_ref, qc, cb_ref[:, qc], ML_CONV))
            k = _silu(_causal_conv(k_pre, cprev_ref[:, kc], cw_ref, kc, cb_ref[:, kc], ML_CONV))
            k = k * (HEAD_DIM ** -0.5)
            cprev_ref[:, qc] = q_pre[CHUNK - SUBLANES:CHUNK]
            cprev_ref[:, kc] = k_pre[CHUNK - SUBLANES:CHUNK]
            vv = proj_ref[rows, base + 2 * ML_WIDTH + h * HEAD_DIM:base + 2 * ML_WIDTH + (h + 1) * HEAD_DIM]
            og = proj_ref[rows, base + 3 * ML_WIDTH + h * HEAD_DIM:base + 3 * ML_WIDTH + (h + 1) * HEAD_DIM]

            kt = k.T
            s = jnp.dot(q.astype(BF16), kt.astype(BF16), preferred_element_type=F32)
            m_h = m[h:h + 1, :]
            mm = jnp.maximum(m_h, zt[:, h:h + 1])
            p = jnp.exp(jnp.where(tril, r_c[h:h + 1, :] - mm, -jnp.inf))
            iw = jnp.exp(m_h - mm)
            lhs = jnp.concatenate([(s * p).astype(BF16), (iw * q).astype(BF16)], axis=1)
            v_aug = jnp.concatenate([vv.astype(BF16), ones_col], axis=1)
            st_prev = mst_ref[h]
            rhs = jnp.concatenate([v_aug, st_prev.astype(BF16)], axis=0)
            nd = jnp.dot(lhs, rhs, preferred_element_type=F32)
            den = nd[:, HEAD_DIM:HEAD_DIM + 1]
            denom = jnp.maximum(jnp.abs(den), jnp.exp(-(zt[:, SUBLANES + h:SUBLANES + h + 1] + mm)))
            hh = nd[:, 0:HEAD_DIM] / denom
            hn = hh * _rms_scale(hh) * hng_ref[:, hc]
            ymix_ref[rows, GM_WIDTH + h * HEAD_DIM:GM_WIDTH + (h + 1) * HEAD_DIM] = (
                jax.nn.sigmoid(og) * hn).astype(BF16)

            kw = (kt * wgt[h:h + 1, :]).astype(BF16)
            st_c = jnp.dot(kw, v_aug, preferred_element_type=F32)
            mst_ref[h] = decay[h:h + 1, :] * st_prev + inw[h:h + 1, :] * st_c
        m = m_new
    m_ref[...] = jnp.broadcast_to(m, m_ref.shape)

    y = jnp.dot(ymix_ref[...], wout_ref[...], preferred_element_type=F32)
    for c in range(nc):
        rows = slice(c * CHUNK, (c + 1) * CHUNK)
        o_ref[0, rows, :] = x_ref[0, rows, :] + g1 * y[rows, :]


def _token_mix(x, mod, n1g, w_main, w_gate_t, gate_b, vng, ws, bs_col, cw, cb, hng, w_out, ts):
    bsz, seq, _ = x.shape
    const2 = lambda b, t: (0, 0)
    return pl.pallas_call(
        functools.partial(_mix_kernel, ts=ts),
        grid=(bsz, seq // ts),
        in_specs=[
            pl.BlockSpec((1, ts, D_MODEL), lambda b, t: (b, t, 0)),
            pl.BlockSpec((1, 8, D_MODEL), lambda b, t: (b, 0, 0)),
            pl.BlockSpec((1, D_MODEL), const2),
            pl.BlockSpec((D_MODEL, N_MAIN), const2),
            pl.BlockSpec((16, D_MODEL), const2),
            pl.BlockSpec((SUBLANES, 128), const2),
            pl.BlockSpec((1, GM_WIDTH), const2),
            pl.BlockSpec((N_HEADS, CHUNK, CHUNK), lambda b, t: (0, 0, 0)),
            pl.BlockSpec((CHUNK, 128), const2),
            pl.BlockSpec((ML_CONV, 2 * ML_WIDTH), const2),
            pl.BlockSpec((1, 2 * ML_WIDTH), const2),
            pl.BlockSpec((1, ML_WIDTH), const2),
            pl.BlockSpec((D_MODEL, D_MODEL), const2),
        ],
        out_specs=pl.BlockSpec((1, ts, D_MODEL), lambda b, t: (b, t, 0)),
        out_shape=jax.ShapeDtypeStruct(x.shape, F32),
        scratch_shapes=[
            pltpu.VMEM((ts, D_MODEL), BF16),
            pltpu.VMEM((ts, N_MAIN), F32),
            pltpu.VMEM((ts, D_MODEL), BF16),
            pltpu.VMEM((SUBLANES, 2 * ML_WIDTH), F32),
            pltpu.VMEM((N_HEADS, HEAD_DIM, 2 * HEAD_DIM), F32),
            pltpu.VMEM((SUBLANES, 128), F32),
        ],
        compiler_params=pltpu.CompilerParams(
            dimension_semantics=("arbitrary", "arbitrary"),
            vmem_limit_bytes=VMEM_LIMIT_BYTES),
        name="token_mix",
    )(x, mod, n1g, w_main, w_gate_t, gate_b, vng, ws, bs_col, cw, cb, hng, w_out)


def _ffn_kernel(x_ref, mod_ref, n2g_ref, wup_ref, fcw_ref, fcb_ref, wdn_ref, fg_ref, o_ref,
                hb_ref, a_ref, gprev_ref, *, ts, nb):
    @pl.when(pl.program_id(1) == 0)
    def _():
        gprev_ref[...] = jnp.zeros_like(gprev_ref)

    mod = mod_ref[0]
    sh2, sc2, g2, fsh, fsc = mod[3:4], mod[4:5], mod[5:6], mod[6:7], mod[7:8]
    nr = ts // CHUNK

    for c in range(nr):
        rows = slice(c * CHUNK, (c + 1) * CHUNK)
        xs = x_ref[0, rows, :]
        y = xs * _rms_scale(xs) * n2g_ref[...]
        hb_ref[rows, :] = (y * (1.0 + sc2) + sh2).astype(BF16)

    for j in range(D_FF // nb):
        cols = slice(j * nb, (j + 1) * nb)
        g = jnp.dot(hb_ref[...], wup_ref[:, cols], preferred_element_type=F32)
        u = jnp.dot(hb_ref[...], wup_ref[:, D_FF + j * nb:D_FF + (j + 1) * nb],
                    preferred_element_type=F32)
        gc = _causal_conv(g, gprev_ref[:, cols], fcw_ref, cols, fcb_ref[:, cols], FFN_CONV)
        gprev_ref[:, cols] = g[ts - SUBLANES:ts]
        a_ref[:, cols] = (_gelu(gc) * u).astype(BF16)

    f = jnp.dot(a_ref[...], wdn_ref[...], preferred_element_type=F32)
    for c in range(nr):
        rows = slice(c * CHUNK, (c + 1) * CHUNK)
        x2 = x_ref[0, rows, :] + g2 * f[rows, :]
        y = x2 * _rms_scale(x2) * fg_ref[...]
        o_ref[0, rows, :] = y * (1.0 + fsc) + fsh


def _channel_mix(x, mod, n2g, w_up, fcw, fcb, w_down, fg, ts, nb):
    bsz, seq, _ = x.shape
    const2 = lambda b, t: (0, 0)
    return pl.pallas_call(
        functools.partial(_ffn_kernel, ts=ts, nb=nb),
        grid=(bsz, seq // ts),
        in_specs=[
            pl.BlockSpec((1, ts, D_MODEL), lambda b, t: (b, t, 0)),
            pl.BlockSpec((1, 8, D_MODEL), lambda b, t: (b, 0, 0)),
            pl.BlockSpec((1, D_MODEL), const2),
            pl.BlockSpec((D_MODEL, 2 * D_FF), const2),
            pl.BlockSpec((FFN_CONV, D_FF), const2),
            pl.BlockSpec((1, D_FF), const2),
            pl.BlockSpec((D_FF, D_MODEL), const2),
            pl.BlockSpec((1, D_MODEL), const2),
        ],
        out_specs=pl.BlockSpec((1, ts, D_MODEL), lambda b, t: (b, t, 0)),
        out_shape=jax.ShapeDtypeStruct(x.shape, F32),
        scratch_shapes=[
            pltpu.VMEM((ts, D_MODEL), BF16),
            pltpu.VMEM((ts, D_FF), BF16),
            pltpu.VMEM((SUBLANES, D_FF), F32),
        ],
        compiler_params=pltpu.CompilerParams(
            dimension_semantics=("arbitrary", "arbitrary"),
            vmem_limit_bytes=VMEM_LIMIT_BYTES),
        name="channel_mix",
    )(x, mod, n2g, w_up, fcw, fcb, w_down, fg)


def kernel(x, c, ada_w, ada_b, norm1_g, w_in, gm_vnorm_g, gm_spatial_w, gm_spatial_b, ml_conv_w,
           ml_conv_b, ml_i_b, ml_f_b, ml_hnorm_g, w_out, norm2_g, ffn_w_up, ffn_conv_w, ffn_conv_b,
           ffn_w_down, final_ada_w, final_ada_b, final_g):
    depth = ada_w.shape[0]
    assert depth == 1, "the final norm is fused into the channel-mix call of a single layer"
    bsz = x.shape[0]
    for l in range(depth):
        w_all = jnp.concatenate([ada_w[l], final_ada_w], axis=1)
        b_all = jnp.concatenate([ada_b[l], final_ada_b])[None, :]
        mod = _modulation(c, w_all, b_all).reshape(bsz, 8, D_MODEL)

        w_main = w_in[l][:, :N_MAIN].astype(BF16)
        w_gate_t = jnp.zeros((16, D_MODEL), F32).at[0:8].set(w_in[l][:, N_MAIN:].T).astype(BF16)
        gate_b = jnp.broadcast_to(jnp.concatenate([ml_i_b[l], ml_f_b[l]])[:, None], (SUBLANES, 128))
        bs_col = jnp.zeros((CHUNK, 128), F32).at[:, 0:N_HEADS].set(gm_spatial_b[l].T)
        x = _token_mix(x, mod, norm1_g[l][None, :], w_main, w_gate_t, gate_b,
                       gm_vnorm_g[l][None, :], gm_spatial_w[l], bs_col, ml_conv_w[l],
                       ml_conv_b[l][None, :], ml_hnorm_g[l][None, :], w_out[l].astype(BF16), ts=256)
        x = _channel_mix(x, mod, norm2_g[l][None, :], ffn_w_up[l].astype(BF16), ffn_conv_w[l],
                         ffn_conv_b[l][None, :], ffn_w_down[l].astype(BF16), final_g[None, :],
                         ts=256, nb=256)
    return x
```

```python
import functools

import jax
import jax.numpy as jnp
from jax import lax
from jax.experimental import pallas as pl
from jax.experimental.pallas import tpu as pltpu

D_MODEL = 1024
CHUNK = 128
HEAD_DIM = 128
N_HEADS = 4
GM_WIDTH = N_HEADS * HEAD_DIM
ML_WIDTH = N_HEADS * HEAD_DIM
ML_CONV = 4
FFN_CONV = 3
D_FF = 2816
EPS = 1e-6
N_MAIN = 2 * GM_WIDTH + 4 * ML_WIDTH
SUBLANES = 8
VMEM_LIMIT_BYTES = 56 * 1024 * 1024

F32 = jnp.float32
BF16 = jnp.bfloat16


def _gelu(x):
    return 0.5 * x * (1.0 + lax.erf(x * 0.7071067811865476))


def _silu(x):
    return x * jax.nn.sigmoid(x)


def _log_sigmoid(x):
    return jnp.minimum(x, 0.0) - jnp.log1p(jnp.exp(-jnp.abs(x)))


def _rms_scale(x):
    return lax.rsqrt(jnp.mean(x * x, axis=-1, keepdims=True) + EPS)


def _causal_conv(cur, prev, w_ref, cols, bias, width):
    n = cur.shape[0]
    acc = cur * w_ref[width - 1:width, cols] + bias
    sub = lax.broadcasted_iota(jnp.int32, (SUBLANES, cur.shape[1]), 0)
    for d in range(1, width):
        rolled = pltpu.roll(cur, d, axis=0)
        head = jnp.where(sub < d, pltpu.roll(prev, d, axis=0), rolled[0:SUBLANES])
        shifted = jnp.concatenate([head, rolled[SUBLANES:n]], axis=0)
        acc = acc + shifted * w_ref[width - 1 - d:width - d, cols]
    return acc


def _mod_kernel(c_ref, w_ref, b_ref, o_ref):
    ca = _silu(c_ref[...]).astype(BF16)
    o_ref[...] = jnp.dot(ca, w_ref[...].astype(BF16), preferred_element_type=F32) + b_ref[...]


def _modulation(c, w_all, b_all):
    bsz = c.shape[0]
    n = w_all.shape[1]
    bn = D_MODEL
    return pl.pallas_call(
        _mod_kernel,
        grid=(n // bn,),
        in_specs=[
            pl.BlockSpec((bsz, D_MODEL), lambda j: (0, 0)),
            pl.BlockSpec((D_MODEL, bn), lambda j: (0, j)),
            pl.BlockSpec((1, bn), lambda j: (0, j)),
        ],
        out_specs=pl.BlockSpec((bsz, bn), lambda j: (0, j)),
        out_shape=jax.ShapeDtypeStruct((bsz, n), F32),
        compiler_params=pltpu.CompilerParams(dimension_semantics=("arbitrary",)),
        name="adaln_modulation",
    )(c, w_all, b_all)


def _mix_kernel(x_ref, mod_ref, n1g_ref, win_ref, wgt_ref, gb_ref, vng_ref, ws_ref, bs_ref,
                cw_ref, cb_ref, hng_ref, wout_ref, o_ref,
                hb_ref, proj_ref, ymix_ref, cprev_ref, mst_ref, m_ref, *, ts):
    nc = ts // CHUNK

    @pl.when(pl.program_id(1) == 0)
    def _():
        cprev_ref[...] = jnp.zeros_like(cprev_ref)
        mst_ref[...] = jnp.zeros_like(mst_ref)
        m_ref[...] = jnp.zeros_like(m_ref)

    mod = mod_ref[0]
    sh1, sc1, g1 = mod[0:1], mod[1:2], mod[2:3]

    for c in range(nc):
        rows = slice(c * CHUNK, (c + 1) * CHUNK)
        xs = x_ref[0, rows, :]
        y = xs * _rms_scale(xs) * n1g_ref[...]
        hb_ref[rows, :] = (y * (1.0 + sc1) + sh1).astype(BF16)

    gates = lax.dot_general(wgt_ref[...], hb_ref[...], (((1,), (1,)), ((), ())),
                            preferred_element_type=F32)
    gates = gates[0:SUBLANES] + gb_ref[:, 0:1]

    ii = gates
    logf = _log_sigmoid(pltpu.roll(gates, N_HEADS, axis=0))
    lane = lax.broadcasted_iota(jnp.int32, (SUBLANES, ts), 1) % CHUNK
    b = logf
    for d in (1, 2, 4, 8, 16, 32, 64):
        b = b + jnp.where(lane >= d, pltpu.roll(b, d, axis=1), 0.0)
    r = ii - b
    cr = r
    for d in (1, 2, 4, 8, 16, 32, 64):
        cr = jnp.maximum(cr, jnp.where(lane >= d, pltpu.roll(cr, d, axis=1), -jnp.inf))

    proj_ref[...] = jnp.dot(hb_ref[...], win_ref[...], preferred_element_type=F32)

    tri_r = lax.broadcasted_iota(jnp.int32, (CHUNK, CHUNK), 0)
    tri_c = lax.broadcasted_iota(jnp.int32, (CHUNK, CHUNK), 1)
    tril = tri_c <= tri_r
    ones_col = jnp.where(tri_c == 0, 1.0, 0.0).astype(BF16)
    ws = [jnp.where(tril, ws_ref[h], 0.0).astype(BF16) for h in range(N_HEADS)]
    zpad = jnp.zeros((CHUNK - 2 * SUBLANES, CHUNK), F32)

    m = m_ref[:, 0:1]
    for c in range(nc):
        rows = slice(c * CHUNK, (c + 1) * CHUNK)
        b_c, r_c, cr_c, ii_c = b[:, rows], r[:, rows], cr[:, rows], ii[:, rows]
        b_last = b_c[:, CHUNK - 1:CHUNK]
        a_c = b_last - b_c + ii_c
        a_max = jnp.max(a_c, axis=1, keepdims=True)
        wgt = jnp.exp(a_c - a_max)
        m_new = jnp.maximum(b_last + m, a_max)
        decay = jnp.exp(b_last + m - m_new)
        inw = jnp.exp(a_max - m_new)
        zt = jnp.concatenate([cr_c, b_c, zpad], axis=0).T

        for h in range(N_HEADS):
            hc = slice(h * HEAD_DIM, (h + 1) * HEAD_DIM)
            u = _gelu(proj_ref[rows, hc])
            v = _gelu(proj_ref[rows, GM_WIDTH + h * HEAD_DIM:GM_WIDTH + (h + 1) * HEAD_DIM])
            v = v * _rms_scale(v) * vng_ref[:, hc]
            mixed = jnp.dot(ws[h], v.astype(BF16), preferred_element_type=F32) + bs_ref[:, h:h + 1]
            ymix_ref[rows, hc] = (u * mixed).astype(BF16)

            qc = slice(h * HEAD_DIM, (h + 1) * HEAD_DIM)
            kc = slice(ML_WIDTH + h * HEAD_DIM, ML_WIDTH + (h + 1) * HEAD_DIM)
            base = 2 * GM_WIDTH
            q_pre = proj_ref[rows, base + qc.start:base + qc.stop]
            k_pre = proj_ref[rows, base + kc.start:base + kc.stop]
            q = _silu(_causal_conv(q_pre, cprev_ref[:, qc], cw_ref, qc, cb_ref[:, qc], ML_CONV))
            k = _silu(_causal_conv(k_pre, cprev_ref[:, kc], cw_ref, kc, cb_ref[:, kc], ML_CONV))
            k = k * (HEAD_DIM ** -0.5)
            cprev_ref[:, qc] = q_pre[CHUNK - SUBLANES:CHUNK]
            cprev_ref[:, kc] = k_pre[CHUNK - SUBLANES:CHUNK]
            vv = proj_ref[rows, base + 2 * ML_WIDTH + h * HEAD_DIM:base + 2 * ML_WIDTH + (h + 1) * HEAD_DIM]
            og = proj_ref[rows, base + 3 * ML_WIDTH + h * HEAD_DIM:base + 3 * ML_WIDTH + (h + 1) * HEAD_DIM]

            kt = k.T
            s = jnp.dot(q.astype(BF16), kt.astype(BF16), preferred_element_type=F32)
            m_h = m[h:h + 1, :]
            mm = jnp.maximum(m_h, zt[:, h:h + 1])
            p = jnp.exp(jnp.where(tril, r_c[h:h + 1, :] - mm, -jnp.inf))
            iw = jnp.exp(m_h - mm)
            lhs = jnp.concatenate([(s * p).astype(BF16), (iw * q).astype(BF16)], axis=1)
            v_aug = jnp.concatenate([vv.astype(BF16), ones_col], axis=1)
            st_prev = mst_ref[h]
            rhs = jnp.concatenate([v_aug, st_prev.astype(BF16)], axis=0)
            nd = jnp.dot(lhs, rhs, preferred_element_type=F32)
            den = nd[:, HEAD_DIM:HEAD_DIM + 1]
            denom = jnp.maximum(jnp.abs(den), jnp.exp(-(zt[:, SUBLANES + h:SUBLANES + h + 1] + mm)))
            hh = nd[:, 0:HEAD_DIM] / denom
            hn = hh * _rms_scale(hh) * hng_ref[:, hc]
            ymix_ref[rows, GM_WIDTH + h * HEAD_DIM:GM_WIDTH + (h + 1) * HEAD_DIM] = (
                jax.nn.sigmoid(og) * hn).astype(BF16)

            kw = (kt * wgt[h:h + 1, :]).astype(BF16)
            st_c = jnp.dot(kw, v_aug, preferred_element_type=F32)
            mst_ref[h] = decay[h:h + 1, :] * st_prev + inw[h:h + 1, :] * st_c
        m = m_new
    m_ref[...] = jnp.broadcast_to(m, m_ref.shape)

    y = jnp.dot(ymix_ref[...], wout_ref[...], preferred_element_type=F32)
    for c in range(nc):
        rows = slice(c * CHUNK, (c + 1) * CHUNK)
        o_ref[0, rows, :] = x_ref[0, rows, :] + g1 * y[rows, :]


def _token_mix(x, mod, n1g, w_main, w_gate_t, gate_b, vng, ws, bs_col, cw, cb, hng, w_out, ts):
    bsz, seq, _ = x.shape
    const2 = lambda b, t: (0, 0)
    return pl.pallas_call(
        functools.partial(_mix_kernel, ts=ts),
        grid=(bsz, seq // ts),
        in_specs=[
            pl.BlockSpec((1, ts, D_MODEL), lambda b, t: (b, t, 0)),
            pl.BlockSpec((1, 8, D_MODEL), lambda b, t: (b, 0, 0)),
            pl.BlockSpec((1, D_MODEL), const2),
            pl.BlockSpec((D_MODEL, N_MAIN), const2),
            pl.BlockSpec((16, D_MODEL), const2),
            pl.BlockSpec((SUBLANES, 128), const2),
            pl.BlockSpec((1, GM_WIDTH), const2),
            pl.BlockSpec((N_HEADS, CHUNK, CHUNK), lambda b, t: (0, 0, 0)),
            pl.BlockSpec((CHUNK, 128), const2),
            pl.BlockSpec((ML_CONV, 2 * ML_WIDTH), const2),
            pl.BlockSpec((1, 2 * ML_WIDTH), const2),
            pl.BlockSpec((1, ML_WIDTH), const2),
            pl.BlockSpec((D_MODEL, D_MODEL), const2),
        ],
        out_specs=pl.BlockSpec((1, ts, D_MODEL), lambda b, t: (b, t, 0)),
        out_shape=jax.ShapeDtypeStruct(x.shape, F32),
        scratch_shapes=[
            pltpu.VMEM((ts, D_MODEL), BF16),
            pltpu.VMEM((ts, N_MAIN), F32),
            pltpu.VMEM((ts, D_MODEL), BF16),
            pltpu.VMEM((SUBLANES, 2 * ML_WIDTH), F32),
            pltpu.VMEM((N_HEADS, HEAD_DIM, 2 * HEAD_DIM), F32),
            pltpu.VMEM((SUBLANES, 128), F32),
        ],
        compiler_params=pltpu.CompilerParams(
            dimension_semantics=("arbitrary", "arbitrary"),
            vmem_limit_bytes=VMEM_LIMIT_BYTES),
        name="token_mix",
    )(x, mod, n1g, w_main, w_gate_t, gate_b, vng, ws, bs_col, cw, cb, hng, w_out)


def _ffn_kernel(x_ref, mod_ref, n2g_ref, wup_ref, fcw_ref, fcb_ref, wdn_ref, fg_ref, o_ref,
                hb_ref, a_ref, gprev_ref, *, ts, nb):
    @pl.when(pl.program_id(1) == 0)
    def _():
        gprev_ref[...] = jnp.zeros_like(gprev_ref)

    mod = mod_ref[0]
    sh2, sc2, g2, fsh, fsc = mod[3:4], mod[4:5], mod[5:6], mod[6:7], mod[7:8]
    nr = ts // CHUNK

    for c in range(nr):
        rows = slice(c * CHUNK, (c + 1) * CHUNK)
        xs = x_ref[0, rows, :]
        y = xs * _rms_scale(xs) * n2g_ref[...]
        hb_ref[rows, :] = (y * (1.0 + sc2) + sh2).astype(BF16)

    for j in range(D_FF // nb):
        cols = slice(j * nb, (j + 1) * nb)
        g = jnp.dot(hb_ref[...], wup_ref[:, cols], preferred_element_type=F32)
        u = jnp.dot(hb_ref[...], wup_ref[:, D_FF + j * nb:D_FF + (j + 1) * nb],
                    preferred_element_type=F32)
        gc = _causal_conv(g, gprev_ref[:, cols], fcw_ref, cols, fcb_ref[:, cols], FFN_CONV)
        gprev_ref[:, cols] = g[ts - SUBLANES:ts]
        a_ref[:, cols] = (_gelu(gc) * u).astype(BF16)

    f = jnp.dot(a_ref[...], wdn_ref[...], preferred_element_type=F32)
    for c in range(nr):
        rows = slice(c * CHUNK, (c + 1) * CHUNK)
        x2 = x_ref[0, rows, :] + g2 * f[rows, :]
        y = x2 * _rms_scale(x2) * fg_ref[...]
        o_ref[0, rows, :] = y * (1.0 + fsc) + fsh


def _channel_mix(x, mod, n2g, w_up, fcw, fcb, w_down, fg, ts, nb):
    bsz, seq, _ = x.shape
    const2 = lambda b, t: (0, 0)
    return pl.pallas_call(
        functools.partial(_ffn_kernel, ts=ts, nb=nb),
        grid=(bsz, seq // ts),
        in_specs=[
            pl.BlockSpec((1, ts, D_MODEL), lambda b, t: (b, t, 0)),
            pl.BlockSpec((1, 8, D_MODEL), lambda b, t: (b, 0, 0)),
            pl.BlockSpec((1, D_MODEL), const2),
            pl.BlockSpec((D_MODEL, 2 * D_FF), const2),
            pl.BlockSpec((FFN_CONV, D_FF), const2),
            pl.BlockSpec((1, D_FF), const2),
            pl.BlockSpec((D_FF, D_MODEL), const2),
            pl.BlockSpec((1, D_MODEL), const2),
        ],
        out_specs=pl.BlockSpec((1, ts, D_MODEL), lambda b, t: (b, t, 0)),
        out_shape=jax.ShapeDtypeStruct(x.shape, F32),
        scratch_shapes=[
            pltpu.VMEM((ts, D_MODEL), BF16),
            pltpu.VMEM((ts, D_FF), BF16),
            pltpu.VMEM((SUBLANES, D_FF), F32),
        ],
        compiler_params=pltpu.CompilerParams(
            dimension_semantics=("arbitrary", "arbitrary"),
            vmem_limit_bytes=VMEM_LIMIT_BYTES),
        name="channel_mix",
    )(x, mod, n2g, w_up, fcw, fcb, w_down, fg)


def kernel(x, c, ada_w, ada_b, norm1_g, w_in, gm_vnorm_g, gm_spatial_w, gm_spatial_b, ml_conv_w,
           ml_conv_b, ml_i_b, ml_f_b, ml_hnorm_g, w_out, norm2_g, ffn_w_up, ffn_conv_w, ffn_conv_b,
           ffn_w_down, final_ada_w, final_ada_b, final_g):
    depth = ada_w.shape[0]
    assert depth == 1, "the final norm is fused into the channel-mix call of a single layer"
    bsz = x.shape[0]
    for l in range(depth):
        w_all = jnp.concatenate([ada_w[l], final_ada_w], axis=1)
        b_all = jnp.concatenate([ada_b[l], final_ada_b])[None, :]
        mod = _modulation(c, w_all, b_all).reshape(bsz, 8, D_MODEL)

        w_main = w_in[l][:, :N_MAIN].astype(BF16)
        w_gate_t = jnp.zeros((16, D_MODEL), F32).at[0:8].set(w_in[l][:, N_MAIN:].T).astype(BF16)
        gate_b = jnp.broadcast_to(jnp.concatenate([ml_i_b[l], ml_f_b[l]])[:, None], (SUBLANES, 128))
        bs_col = jnp.zeros((CHUNK, 128), F32).at[:, 0:N_HEADS].set(gm_spatial_b[l].T)
        x = _token_mix(x, mod, norm1_g[l][None, :], w_main, w_gate_t, gate_b,
                       gm_vnorm_g[l][None, :], gm_spatial_w[l], bs_col, ml_conv_w[l],
                       ml_conv_b[l][None, :], ml_hnorm_g[l][None, :], w_out[l].astype(BF16), ts=512)
        x = _channel_mix(x, mod, norm2_g[l][None, :], ffn_w_up[l].astype(BF16), ffn_conv_w[l],
                         ffn_conv_b[l][None, :], ffn_w_down[l].astype(BF16), final_g[None, :],
                         ts=256, nb=256)
    return x
```

```python
import functools

import jax
import jax.numpy as jnp
from jax import lax
from jax.experimental import pallas as pl
from jax.experimental.pallas import tpu as pltpu

D_MODEL = 1024
CHUNK = 128
HEAD_DIM = 128
N_HEADS = 4
GM_WIDTH = N_HEADS * HEAD_DIM
ML_WIDTH = N_HEADS * HEAD_DIM
ML_CONV = 4
FFN_CONV = 3
D_FF = 2816
EPS = 1e-6
N_MAIN = 2 * GM_WIDTH + 4 * ML_WIDTH
QK_BASE = 2 * GM_WIDTH
SUBLANES = 8
HIST = SUBLANES
VMEM_LIMIT_BYTES = 56 * 1024 * 1024

F32 = jnp.float32
BF16 = jnp.bfloat16


def _gelu(x):
    return 0.5 * x * (1.0 + lax.erf(x * 0.7071067811865476))


def _silu(x):
    return x * jax.nn.sigmoid(x)


def _log_sigmoid(x):
    return jnp.minimum(x, 0.0) - jnp.log1p(jnp.exp(-jnp.abs(x)))


def _rms_scale(x):
    return lax.rsqrt(jnp.mean(x * x, axis=-1, keepdims=True) + EPS)


def _causal_conv(ref, row0, nrows, cols, w_ref, wcols, bias, width):
    acc = ref[row0:row0 + nrows, cols] * w_ref[width - 1:width, wcols] + bias
    for d in range(1, width):
        acc = acc + ref[row0 - d:row0 - d + nrows, cols] * w_ref[width - 1 - d:width - d, wcols]
    return acc


def _mod_kernel(c_ref, w_ref, b_ref, o_ref):
    ca = _silu(c_ref[...]).astype(BF16)
    o_ref[...] = jnp.dot(ca, w_ref[...].astype(BF16), preferred_element_type=F32) + b_ref[...]


def _modulation(c, w, b):
    bsz = c.shape[0]
    n = w.shape[1]
    bn = D_MODEL
    return pl.pallas_call(
        _mod_kernel,
        grid=(n // bn,),
        in_specs=[
            pl.BlockSpec((bsz, D_MODEL), lambda j: (0, 0)),
            pl.BlockSpec((D_MODEL, bn), lambda j: (0, j)),
            pl.BlockSpec((1, bn), lambda j: (0, j)),
        ],
        out_specs=pl.BlockSpec((bsz, bn), lambda j: (0, j)),
        out_shape=jax.ShapeDtypeStruct((bsz, n), F32),
        compiler_params=pltpu.CompilerParams(dimension_semantics=("arbitrary",)),
        name="adaln_modulation",
    )(c, w, b)


def _mix_kernel(x_ref, mod_ref, n1g_ref, win_ref, wgt_ref, gb_ref, vng_ref, ws_ref, bs_ref,
                cw_ref, cb_ref, hng_ref, wout_ref, o_ref,
                hb_ref, proj_ref, ymix_ref, wsb_ref, mst_ref, m_ref, *, ts):
    nc = ts // CHUNK
    tri_r = lax.broadcasted_iota(jnp.int32, (CHUNK, CHUNK), 0)
    tri_c = lax.broadcasted_iota(jnp.int32, (CHUNK, CHUNK), 1)
    tril = tri_c <= tri_r

    @pl.when(pl.program_id(1) == 0)
    def _():
        proj_ref[0:HIST, QK_BASE:QK_BASE + 2 * ML_WIDTH] = jnp.zeros((HIST, 2 * ML_WIDTH), F32)
        mst_ref[...] = jnp.zeros_like(mst_ref)
        m_ref[...] = jnp.zeros_like(m_ref)
        for h in range(N_HEADS):
            wsb_ref[h] = jnp.where(tril, ws_ref[h], 0.0).astype(BF16)

    mod = mod_ref[0]
    sh1, sc1, g1 = mod[0:1], mod[1:2], mod[2:3]

    for c in range(nc):
        rows = slice(c * CHUNK, (c + 1) * CHUNK)
        xs = x_ref[0, rows, :]
        y = xs * _rms_scale(xs) * n1g_ref[...]
        hb_ref[rows, :] = (y * (1.0 + sc1) + sh1).astype(BF16)

    gates = lax.dot_general(wgt_ref[...], hb_ref[...], (((1,), (1,)), ((), ())),
                            preferred_element_type=F32)
    gates = gates[0:SUBLANES] + gb_ref[:, 0:1]

    ii = gates
    logf = _log_sigmoid(pltpu.roll(gates, N_HEADS, axis=0))
    lane = lax.broadcasted_iota(jnp.int32, (SUBLANES, ts), 1) % CHUNK
    b = logf
    for d in (1, 2, 4, 8, 16, 32, 64):
        b = b + jnp.where(lane >= d, pltpu.roll(b, d, axis=1), 0.0)
    r = ii - b
    cr = r
    for d in (1, 2, 4, 8, 16, 32, 64):
        cr = jnp.maximum(cr, jnp.where(lane >= d, pltpu.roll(cr, d, axis=1), -jnp.inf))

    zpad = jnp.zeros((CHUNK - 3 * SUBLANES, CHUNK), F32)
    m = m_ref[:, 0:1]
    zts, wgts, decays = [], [], []
    for c in range(nc):
        rows = slice(c * CHUNK, (c + 1) * CHUNK)
        b_c, cr_c, ii_c = b[:, rows], cr[:, rows], ii[:, rows]
        b_last = b_c[:, CHUNK - 1:CHUNK]
        a_c = b_last - b_c + ii_c
        a_max = jnp.max(a_c, axis=1, keepdims=True)
        m_new = jnp.maximum(b_last + m, a_max)
        decays.append(jnp.exp(b_last + m - m_new))
        wgts.append(jnp.exp(a_c - a_max) * jnp.exp(a_max - m_new))
        mm = jnp.maximum(m, cr_c)
        zts.append(jnp.concatenate([mm, jnp.exp(m - mm), jnp.exp(-(b_c + mm)), zpad], axis=0).T)
        m = m_new
    m_ref[...] = jnp.broadcast_to(m, m_ref.shape)

    proj_ref[HIST:HIST + ts, :] = jnp.dot(hb_ref[...], win_ref[...], preferred_element_type=F32)

    ones_col = jnp.where(tri_c == 0, 1.0, 0.0).astype(BF16)

    for c in range(nc):
        rows = slice(c * CHUNK, (c + 1) * CHUNK)
        prow = slice(HIST + c * CHUNK, HIST + (c + 1) * CHUNK)
        zt, wgt, decay = zts[c], wgts[c], decays[c]
        r_c = r[:, rows]

        for h in range(N_HEADS):
            hc = slice(h * HEAD_DIM, (h + 1) * HEAD_DIM)
            u = _gelu(proj_ref[prow, hc])
            v = _gelu(proj_ref[prow, GM_WIDTH + h * HEAD_DIM:GM_WIDTH + (h + 1) * HEAD_DIM])
            v = v * _rms_scale(v) * vng_ref[:, hc]
            mixed = jnp.dot(wsb_ref[h], v.astype(BF16), preferred_element_type=F32) + bs_ref[:, h:h + 1]
            ymix_ref[rows, hc] = (u * mixed).astype(BF16)

            qc = slice(h * HEAD_DIM, (h + 1) * HEAD_DIM)
            kc = slice(ML_WIDTH + h * HEAD_DIM, ML_WIDTH + (h + 1) * HEAD_DIM)
            q = _silu(_causal_conv(proj_ref, prow.start, CHUNK,
                                   slice(QK_BASE + qc.start, QK_BASE + qc.stop),
                                   cw_ref, qc, cb_ref[:, qc], ML_CONV))
            k = _silu(_causal_conv(proj_ref, prow.start, CHUNK,
                                   slice(QK_BASE + kc.start, QK_BASE + kc.stop),
                                   cw_ref, kc, cb_ref[:, kc], ML_CONV))
            k = k * (HEAD_DIM ** -0.5)
            vbase = QK_BASE + 2 * ML_WIDTH + h * HEAD_DIM
            obase = QK_BASE + 3 * ML_WIDTH + h * HEAD_DIM
            vv = proj_ref[prow, vbase:vbase + HEAD_DIM]
            og = proj_ref[prow, obase:obase + HEAD_DIM]

            kt = k.T
            s = jnp.dot(q.astype(BF16), kt.astype(BF16), preferred_element_type=F32)
            mm = zt[:, h:h + 1]
            iw = zt[:, SUBLANES + h:SUBLANES + h + 1]
            en = zt[:, 2 * SUBLANES + h:2 * SUBLANES + h + 1]
            p = jnp.exp(jnp.where(tril, r_c[h:h + 1, :] - mm, -jnp.inf))
            lhs = jnp.concatenate([(s * p).astype(BF16), (iw * q).astype(BF16)], axis=1)
            v_aug = jnp.concatenate([vv.astype(BF16), ones_col], axis=1)
            st_prev = mst_ref[h]
            rhs = jnp.concatenate([v_aug, st_prev.astype(BF16)], axis=0)
            nd = jnp.dot(lhs, rhs, preferred_element_type=F32)
            den = nd[:, HEAD_DIM:HEAD_DIM + 1]
            hh = nd[:, 0:HEAD_DIM] / jnp.maximum(jnp.abs(den), en)
            hn = hh * _rms_scale(hh) * hng_ref[:, hc]
            ymix_ref[rows, GM_WIDTH + h * HEAD_DIM:GM_WIDTH + (h + 1) * HEAD_DIM] = (
                jax.nn.sigmoid(og) * hn).astype(BF16)

            kw = (kt * wgt[h:h + 1, :]).astype(BF16)
            st_c = jnp.dot(kw, v_aug, preferred_element_type=F32)
            mst_ref[h] = decay[h:h + 1, :] * st_prev + st_c

    proj_ref[0:HIST, QK_BASE:QK_BASE + 2 * ML_WIDTH] = (
        proj_ref[ts:ts + HIST, QK_BASE:QK_BASE + 2 * ML_WIDTH])

    y = jnp.dot(ymix_ref[...], wout_ref[...], preferred_element_type=F32)
    for c in range(nc):
        rows = slice(c * CHUNK, (c + 1) * CHUNK)
        o_ref[0, rows, :] = x_ref[0, rows, :] + g1 * y[rows, :]


def _token_mix(x, mod, n1g, w_main, w_gate_t, gate_b, vng, ws, bs_col, cw, cb, hng, w_out, ts):
    bsz, seq, _ = x.shape
    const2 = lambda b, t: (0, 0)
    return pl.pallas_call(
        functools.partial(_mix_kernel, ts=ts),
        grid=(bsz, seq // ts),
        in_specs=[
            pl.BlockSpec((1, ts, D_MODEL), lambda b, t: (b, t, 0)),
            pl.BlockSpec((1, 6, D_MODEL), lambda b, t: (b, 0, 0)),
            pl.BlockSpec((1, D_MODEL), const2),
            pl.BlockSpec((D_MODEL, N_MAIN), const2),
            pl.BlockSpec((16, D_MODEL), const2),
            pl.BlockSpec((SUBLANES, 128), const2),
            pl.BlockSpec((1, GM_WIDTH), const2),
            pl.BlockSpec((N_HEADS, CHUNK, CHUNK), lambda b, t: (0, 0, 0)),
            pl.BlockSpec((CHUNK, 128), const2),
            pl.BlockSpec((ML_CONV, 2 * ML_WIDTH), const2),
            pl.BlockSpec((1, 2 * ML_WIDTH), const2),
            pl.BlockSpec((1, ML_WIDTH), const2),
            pl.BlockSpec((D_MODEL, D_MODEL), const2),
        ],
        out_specs=pl.BlockSpec((1, ts, D_MODEL), lambda b, t: (b, t, 0)),
        out_shape=jax.ShapeDtypeStruct(x.shape, F32),
        scratch_shapes=[
            pltpu.VMEM((ts, D_MODEL), BF16),
            pltpu.VMEM((HIST + ts, N_MAIN), F32),
            pltpu.VMEM((ts, D_MODEL), BF16),
            pltpu.VMEM((N_HEADS, CHUNK, CHUNK), BF16),
            pltpu.VMEM((N_HEADS, HEAD_DIM, 2 * HEAD_DIM), F32),
            pltpu.VMEM((SUBLANES, 128), F32),
        ],
        compiler_params=pltpu.CompilerParams(
            dimension_semantics=("arbitrary", "arbitrary"),
            vmem_limit_bytes=VMEM_LIMIT_BYTES),
        name="token_mix",
    )(x, mod, n1g, w_main, w_gate_t, gate_b, vng, ws, bs_col, cw, cb, hng, w_out)


def _ffn_kernel(x_ref, mod_ref, fmod_ref, n2g_ref, wup_ref, fcw_ref, fcb_ref, wdn_ref, fg_ref, o_ref,
                hb_ref, g_ref, a_ref, *, ts, nb):
    @pl.when(pl.program_id(1) == 0)
    def _():
        g_ref[0:HIST, :] = jnp.zeros((HIST, D_FF), F32)

    mod = mod_ref[0]
    sh2, sc2, g2 = mod[3:4], mod[4:5], mod[5:6]
    fmod = fmod_ref[0]
    fsh, fsc = fmod[0:1], fmod[1:2]
    nr = ts // CHUNK

    for c in range(nr):
        rows = slice(c * CHUNK, (c + 1) * CHUNK)
        xs = x_ref[0, rows, :]
        y = xs * _rms_scale(xs) * n2g_ref[...]
        hb_ref[rows, :] = (y * (1.0 + sc2) + sh2).astype(BF16)

    for j in range(D_FF // nb):
        cols = slice(j * nb, (j + 1) * nb)
        g_ref[HIST:HIST + ts, cols] = jnp.dot(hb_ref[...], wup_ref[:, cols],
                                              preferred_element_type=F32)
        u = jnp.dot(hb_ref[...], wup_ref[:, D_FF + j * nb:D_FF + (j + 1) * nb],
                    preferred_element_type=F32)
        gc = _causal_conv(g_ref, HIST, ts, cols, fcw_ref, cols, fcb_ref[:, cols], FFN_CONV)
        a_ref[:, cols] = (_gelu(gc) * u).astype(BF16)

    g_ref[0:HIST, :] = g_ref[ts:ts + HIST, :]

    f = jnp.dot(a_ref[...], wdn_ref[...], preferred_element_type=F32)
    for c in range(nr):
        rows = slice(c * CHUNK, (c + 1) * CHUNK)
        x2 = x_ref[0, rows, :] + g2 * f[rows, :]
        y = x2 * _rms_scale(x2) * fg_ref[...]
        o_ref[0, rows, :] = y * (1.0 + fsc) + fsh


def _channel_mix(x, mod, fmod, n2g, w_up, fcw, fcb, w_down, fg, ts, nb):
    bsz, seq, _ = x.shape
    const2 = lambda b, t: (0, 0)
    return pl.pallas_call(
        functools.partial(_ffn_kernel, ts=ts, nb=nb),
        grid=(bsz, seq // ts),
        in_specs=[
            pl.BlockSpec((1, ts, D_MODEL), lambda b, t: (b, t, 0)),
            pl.BlockSpec((1, 6, D_MODEL), lambda b, t: (b, 0, 0)),
            pl.BlockSpec((1, 2, D_MODEL), lambda b, t: (b, 0, 0)),
            pl.BlockSpec((1, D_MODEL), const2),
            pl.BlockSpec((D_MODEL, 2 * D_FF), const2),
            pl.BlockSpec((FFN_CONV, D_FF), const2),
            pl.BlockSpec((1, D_FF), const2),
            pl.BlockSpec((D_FF, D_MODEL), const2),
            pl.BlockSpec((1, D_MODEL), const2),
        ],
        out_specs=pl.BlockSpec((1, ts, D_MODEL), lambda b, t: (b, t, 0)),
        out_shape=jax.ShapeDtypeStruct(x.shape, F32),
        scratch_shapes=[
            pltpu.VMEM((ts, D_MODEL), BF16),
            pltpu.VMEM((HIST + ts, D_FF), F32),
            pltpu.VMEM((ts, D_FF), BF16),
        ],
        compiler_params=pltpu.CompilerParams(
            dimension_semantics=("arbitrary", "arbitrary"),
            vmem_limit_bytes=VMEM_LIMIT_BYTES),
        name="channel_mix",
    )(x, mod, fmod, n2g, w_up, fcw, fcb, w_down, fg)


def kernel(x, c, ada_w, ada_b, norm1_g, w_in, gm_vnorm_g, gm_spatial_w, gm_spatial_b, ml_conv_w,
           ml_conv_b, ml_i_b, ml_f_b, ml_hnorm_g, w_out, norm2_g, ffn_w_up, ffn_conv_w, ffn_conv_b,
           ffn_w_down, final_ada_w, final_ada_b, final_g):
    depth = ada_w.shape[0]
    assert depth == 1, "the final norm is fused into the channel-mix call of a single layer"
    bsz = x.shape[0]
    fmod = _modulation(c, final_ada_w, final_ada_b[None, :]).reshape(bsz, 2, D_MODEL)
    for l in range(depth):
        mod = _modulation(c, ada_w[l], ada_b[l][None, :]).reshape(bsz, 6, D_MODEL)

        w_main = w_in[l][:, :N_MAIN].astype(BF16)
        w_gate_t = jnp.zeros((16, D_MODEL), F32).at[0:8].set(w_in[l][:, N_MAIN:].T).astype(BF16)
        gate_b = jnp.broadcast_to(jnp.concatenate([ml_i_b[l], ml_f_b[l]])[:, None], (SUBLANES, 128))
        bs_col = jnp.zeros((CHUNK, 128), F32).at[:, 0:N_HEADS].set(gm_spatial_b[l].T)
        x = _token_mix(x, mod, norm1_g[l][None, :], w_main, w_gate_t, gate_b,
                       gm_vnorm_g[l][None, :], gm_spatial_w[l], bs_col, ml_conv_w[l],
                       ml_conv_b[l][None, :], ml_hnorm_g[l][None, :], w_out[l].astype(BF16), ts=512)
        x = _channel_mix(x, mod, fmod, norm2_g[l][None, :], ffn_w_up[l].astype(BF16), ffn_conv_w[l],
                         ffn_conv_b[l][None, :], ffn_w_down[l].astype(BF16), final_g[None, :],
                         ts=512, nb=256)
    return x
```

```python
import functools

import jax
import jax.numpy as jnp
from jax import lax
from jax.experimental import pallas as pl
from jax.experimental.pallas import tpu as pltpu

D_MODEL = 1024
CHUNK = 128
HEAD_DIM = 128
N_HEADS = 4
GM_WIDTH = N_HEADS * HEAD_DIM
ML_WIDTH = N_HEADS * HEAD_DIM
ML_CONV = 4
FFN_CONV = 3
D_FF = 2816
EPS = 1e-6
N_MAIN = 2 * GM_WIDTH + 4 * ML_WIDTH
QK_BASE = 2 * GM_WIDTH
SUBLANES = 8
HIST = SUBLANES
VMEM_LIMIT_BYTES = 56 * 1024 * 1024

F32 = jnp.float32
BF16 = jnp.bfloat16


def _gelu(x):
    return 0.5 * x * (1.0 + lax.erf(x * 0.7071067811865476))


def _silu(x):
    return x * jax.nn.sigmoid(x)


def _log_sigmoid(x):
    return jnp.minimum(x, 0.0) - jnp.log1p(jnp.exp(-jnp.abs(x)))


def _rms_scale(x):
    return lax.rsqrt(jnp.mean(x * x, axis=-1, keepdims=True) + EPS)


def _causal_conv(ref, row0, nrows, cols, w_ref, wcols, bias, width):
    acc = ref[row0:row0 + nrows, cols] * w_ref[width - 1:width, wcols] + bias
    for d in range(1, width):
        acc = acc + ref[row0 - d:row0 - d + nrows, cols] * w_ref[width - 1 - d:width - d, wcols]
    return acc


def _mod_kernel(c_ref, w_ref, b_ref, o_ref):
    ca = _silu(c_ref[...]).astype(BF16)
    o_ref[...] = jnp.dot(ca, w_ref[...].astype(BF16), preferred_element_type=F32) + b_ref[...]


def _modulation(c, w, b):
    bsz = c.shape[0]
    n = w.shape[1]
    bn = D_MODEL
    return pl.pallas_call(
        _mod_kernel,
        grid=(n // bn,),
        in_specs=[
            pl.BlockSpec((bsz, D_MODEL), lambda j: (0, 0)),
            pl.BlockSpec((D_MODEL, bn), lambda j: (0, j)),
            pl.BlockSpec((1, bn), lambda j: (0, j)),
        ],
        out_specs=pl.BlockSpec((bsz, bn), lambda j: (0, j)),
        out_shape=jax.ShapeDtypeStruct((bsz, n), F32),
        compiler_params=pltpu.CompilerParams(dimension_semantics=("arbitrary",)),
        name="adaln_modulation",
    )(c, w, b)


PROJ_BLOCK = 256


def _mix_kernel(xa_ref, xc_ref, moda_ref, modc_ref, n1g_ref, win_ref, wgt_ref, gb_ref, vng_ref,
                ws_ref, bs_ref, cw_ref, cb_ref, hng_ref, wout_ref, o_ref,
                hb_ref, proj0_ref, proj1_ref, gr0_ref, gr1_ref, ymix0_ref, ymix1_ref, wsb_ref,
                mst_ref, m_ref, *, ts, nt, n_tiles):
    nc = ts // CHUNK
    s = pl.program_id(0)
    t_a = lax.rem(jnp.minimum(s, n_tiles - 1), nt)
    t_b = lax.rem(jnp.clip(s - 1, 0, n_tiles - 1), nt)
    qk_cols = slice(QK_BASE, QK_BASE + 2 * ML_WIDTH)

    tri_r = lax.broadcasted_iota(jnp.int32, (CHUNK, CHUNK), 0)
    tri_c = lax.broadcasted_iota(jnp.int32, (CHUNK, CHUNK), 1)
    tril = tri_c <= tri_r

    @pl.when(s == 0)
    def _():
        proj1_ref[...] = jnp.zeros_like(proj1_ref)
        gr1_ref[...] = jnp.zeros_like(gr1_ref)
        ymix1_ref[...] = jnp.zeros_like(ymix1_ref)
        for h in range(N_HEADS):
            wsb_ref[h] = jnp.where(tril, ws_ref[h], 0.0).astype(BF16)

    @pl.when(t_b == 0)
    def _():
        mst_ref[...] = jnp.zeros_like(mst_ref)
        m_ref[...] = jnp.zeros_like(m_ref)

    def body(proj_w, proj_r, gr_w, gr_r, ymix_w, ymix_r):
        moda = moda_ref[0]
        sh1, sc1 = moda[0:1], moda[1:2]
        for c in range(nc):
            rows = slice(c * CHUNK, (c + 1) * CHUNK)
            xs = xa_ref[0, rows, :]
            y = xs * _rms_scale(xs) * n1g_ref[...]
            hb_ref[rows, :] = (y * (1.0 + sc1) + sh1).astype(BF16)

        gates = lax.dot_general(wgt_ref[...], hb_ref[...], (((1,), (1,)), ((), ())),
                                preferred_element_type=F32)
        gates = gates[0:SUBLANES] + gb_ref[:, 0:1]
        logf = _log_sigmoid(pltpu.roll(gates, N_HEADS, axis=0))
        lane = lax.broadcasted_iota(jnp.int32, (SUBLANES, ts), 1) % CHUNK
        bsum = logf
        for d in (1, 2, 4, 8, 16, 32, 64):
            bsum = bsum + jnp.where(lane >= d, pltpu.roll(bsum, d, axis=1), 0.0)
        rr = gates - bsum
        crun = rr
        for d in (1, 2, 4, 8, 16, 32, 64):
            crun = jnp.maximum(crun, jnp.where(lane >= d, pltpu.roll(crun, d, axis=1), -jnp.inf))
        gr_w[0] = gates
        gr_w[1] = bsum
        gr_w[2] = rr
        gr_w[3] = crun

        hist = proj_r[ts:ts + HIST, qk_cols]
        proj_w[0:HIST, qk_cols] = jnp.where(t_a == 0, jnp.zeros_like(hist), hist)

        g1 = modc_ref[0][2:3]
        ii, b, r, cr = gr_r[0], gr_r[1], gr_r[2], gr_r[3]
        zpad = jnp.zeros((CHUNK - 3 * SUBLANES, CHUNK), F32)
        m = m_ref[:, 0:1]
        zts, wgts, decays = [], [], []
        for c in range(nc):
            rows = slice(c * CHUNK, (c + 1) * CHUNK)
            b_c, cr_c, ii_c = b[:, rows], cr[:, rows], ii[:, rows]
            b_last = b_c[:, CHUNK - 1:CHUNK]
            a_c = b_last - b_c + ii_c
            a_max = jnp.max(a_c, axis=1, keepdims=True)
            m_new = jnp.maximum(b_last + m, a_max)
            decays.append(jnp.exp(b_last + m - m_new))
            wgts.append(jnp.exp(a_c - a_max) * jnp.exp(a_max - m_new))
            mm = jnp.maximum(m, cr_c)
            zts.append(jnp.concatenate([mm, jnp.exp(m - mm), jnp.exp(-(b_c + mm)), zpad], axis=0).T)
            m = m_new
        m_ref[...] = jnp.broadcast_to(m, m_ref.shape)

        ones_col = jnp.where(tri_c == 0, 1.0, 0.0).astype(BF16)

        n_in = N_MAIN // PROJ_BLOCK
        n_blocks = n_in + D_MODEL // PROJ_BLOCK
        n_items = nc * N_HEADS
        order = []
        for grp in range(D_MODEL // PROJ_BLOCK):
            order.append(n_in + grp)
            order.extend(range(grp * 3, grp * 3 + 3))
        assert sorted(order) == list(range(n_blocks))
        item = 0

        def issue_block(j):
            if j < n_in:
                pc = slice(j * PROJ_BLOCK, (j + 1) * PROJ_BLOCK)
                proj_w[HIST:HIST + ts, pc] = jnp.dot(hb_ref[...], win_ref[:, pc],
                                                     preferred_element_type=F32)
            else:
                oc = slice((j - n_in) * PROJ_BLOCK, (j - n_in + 1) * PROJ_BLOCK)
                y = jnp.dot(ymix_r[...], wout_ref[:, oc], preferred_element_type=F32)
                o_ref[0, :, oc] = xc_ref[0, :, oc] + g1[:, oc] * y

        for c in range(nc):
            rows = slice(c * CHUNK, (c + 1) * CHUNK)
            prow = slice(HIST + c * CHUNK, HIST + (c + 1) * CHUNK)
            zt, wgt, decay = zts[c], wgts[c], decays[c]
            r_c = r[:, rows]

            for h in range(N_HEADS):
                blk_lo = (item * n_blocks + n_items - 1) // n_items
                blk_hi = ((item + 1) * n_blocks + n_items - 1) // n_items
                for j in range(blk_lo, blk_hi):
                    issue_block(order[j])
                item += 1

                hc = slice(h * HEAD_DIM, (h + 1) * HEAD_DIM)
                u = _gelu(proj_r[prow, hc])
                v = _gelu(proj_r[prow, GM_WIDTH + h * HEAD_DIM:GM_WIDTH + (h + 1) * HEAD_DIM])
                v = v * _rms_scale(v) * vng_ref[:, hc]
                mixed = (jnp.dot(wsb_ref[h], v.astype(BF16), preferred_element_type=F32)
                         + bs_ref[:, h:h + 1])
                ymix_w[rows, hc] = (u * mixed).astype(BF16)

                qc = slice(h * HEAD_DIM, (h + 1) * HEAD_DIM)
                kc = slice(ML_WIDTH + h * HEAD_DIM, ML_WIDTH + (h + 1) * HEAD_DIM)
                q = _silu(_causal_conv(proj_r, prow.start, CHUNK,
                                       slice(QK_BASE + qc.start, QK_BASE + qc.stop),
                                       cw_ref, qc, cb_ref[:, qc], ML_CONV))
                k = _silu(_causal_conv(proj_r, prow.start, CHUNK,
                                       slice(QK_BASE + kc.start, QK_BASE + kc.stop),
                                       cw_ref, kc, cb_ref[:, kc], ML_CONV))
                k = k * (HEAD_DIM ** -0.5)
                vbase = QK_BASE + 2 * ML_WIDTH + h * HEAD_DIM
                obase = QK_BASE + 3 * ML_WIDTH + h * HEAD_DIM
                vv = proj_r[prow, vbase:vbase + HEAD_DIM]
                og = proj_r[prow, obase:obase + HEAD_DIM]

                kt = k.T
                sc = jnp.dot(q.astype(BF16), kt.astype(BF16), preferred_element_type=F32)
                mm = zt[:, h:h + 1]
                iw = zt[:, SUBLANES + h:SUBLANES + h + 1]
                en = zt[:, 2 * SUBLANES + h:2 * SUBLANES + h + 1]
                p = jnp.exp(jnp.where(tril, r_c[h:h + 1, :] - mm, -jnp.inf))
                lhs = jnp.concatenate([(sc * p).astype(BF16), (iw * q).astype(BF16)], axis=1)
                v_aug = jnp.concatenate([vv.astype(BF16), ones_col], axis=1)
                st_prev = mst_ref[h]
                rhs = jnp.concatenate([v_aug, st_prev.astype(BF16)], axis=0)
                nd = jnp.dot(lhs, rhs, preferred_element_type=F32)
                den = nd[:, HEAD_DIM:HEAD_DIM + 1]
                hh = nd[:, 0:HEAD_DIM] / jnp.maximum(jnp.abs(den), en)
                hn = hh * _rms_scale(hh) * hng_ref[:, hc]
                ymix_w[rows, GM_WIDTH + h * HEAD_DIM:GM_WIDTH + (h + 1) * HEAD_DIM] = (
                    jax.nn.sigmoid(og) * hn).astype(BF16)

                kw = (kt * wgt[h:h + 1, :]).astype(BF16)
                st_c = jnp.dot(kw, v_aug, preferred_element_type=F32)
                mst_ref[h] = decay[h:h + 1, :] * st_prev + st_c

    parity = lax.rem(s, 2)

    @pl.when(parity == 0)
    def _():
        body(proj0_ref, proj1_ref, gr0_ref, gr1_ref, ymix0_ref, ymix1_ref)

    @pl.when(parity == 1)
    def _():
        body(proj1_ref, proj0_ref, gr1_ref, gr0_ref, ymix1_ref, ymix0_ref)


def _token_mix(x, mod, n1g, w_main, w_gate_t, gate_b, vng, ws, bs_col, cw, cb, hng, w_out, ts):
    bsz, seq, _ = x.shape
    nt = seq // ts
    n_tiles = bsz * nt
    const2 = lambda s: (0, 0)

    def tile_a(s):
        j = jnp.minimum(s, n_tiles - 1)
        return (j // nt, j % nt, 0)

    def tile_c(s):
        j = jnp.maximum(s - 2, 0)
        return (j // nt, j % nt, 0)

    return pl.pallas_call(
        functools.partial(_mix_kernel, ts=ts, nt=nt, n_tiles=n_tiles),
        grid=(n_tiles + 2,),
        in_specs=[
            pl.BlockSpec((1, ts, D_MODEL), tile_a),
            pl.BlockSpec((1, ts, D_MODEL), tile_c),
            pl.BlockSpec((1, 6, D_MODEL), lambda s: (tile_a(s)[0], 0, 0)),
            pl.BlockSpec((1, 6, D_MODEL), lambda s: (tile_c(s)[0], 0, 0)),
            pl.BlockSpec((1, D_MODEL), const2),
            pl.BlockSpec((D_MODEL, N_MAIN), const2),
            pl.BlockSpec((16, D_MODEL), const2),
            pl.BlockSpec((SUBLANES, 128), const2),
            pl.BlockSpec((1, GM_WIDTH), const2),
            pl.BlockSpec((N_HEADS, CHUNK, CHUNK), lambda s: (0, 0, 0)),
            pl.BlockSpec((CHUNK, 128), const2),
            pl.BlockSpec((ML_CONV, 2 * ML_WIDTH), const2),
            pl.BlockSpec((1, 2 * ML_WIDTH), const2),
            pl.BlockSpec((1, ML_WIDTH), const2),
            pl.BlockSpec((D_MODEL, D_MODEL), const2),
        ],
        out_specs=pl.BlockSpec((1, ts, D_MODEL), tile_c),
        out_shape=jax.ShapeDtypeStruct(x.shape, F32),
        scratch_shapes=[
            pltpu.VMEM((ts, D_MODEL), BF16),
            pltpu.VMEM((HIST + ts, N_MAIN), F32),
            pltpu.VMEM((HIST + ts, N_MAIN), F32),
            pltpu.VMEM((4, SUBLANES, ts), F32),
            pltpu.VMEM((4, SUBLANES, ts), F32),
            pltpu.VMEM((ts, D_MODEL), BF16),
            pltpu.VMEM((ts, D_MODEL), BF16),
            pltpu.VMEM((N_HEADS, CHUNK, CHUNK), BF16),
            pltpu.VMEM((N_HEADS, HEAD_DIM, 2 * HEAD_DIM), F32),
            pltpu.VMEM((SUBLANES, 128), F32),
        ],
        compiler_params=pltpu.CompilerParams(
            dimension_semantics=("arbitrary",),
            vmem_limit_bytes=VMEM_LIMIT_BYTES),
        name="token_mix",
    )(x, x, mod, mod, n1g, w_main, w_gate_t, gate_b, vng, ws, bs_col, cw, cb, hng, w_out)


def _ffn_kernel(x_ref, mod_ref, fmod_ref, n2g_ref, wup_ref, fcw_ref, fcb_ref, wdn_ref, fg_ref, o_ref,
                hb_ref, g_ref, a_ref, *, ts, nb):
    @pl.when(pl.program_id(1) == 0)
    def _():
        g_ref[0:HIST, :] = jnp.zeros((HIST, D_FF), F32)

    mod = mod_ref[0]
    sh2, sc2, g2 = mod[3:4], mod[4:5], mod[5:6]
    fmod = fmod_ref[0]
    fsh, fsc = fmod[0:1], fmod[1:2]
    nr = ts // CHUNK

    for c in range(nr):
        rows = slice(c * CHUNK, (c + 1) * CHUNK)
        xs = x_ref[0, rows, :]
        y = xs * _rms_scale(xs) * n2g_ref[...]
        hb_ref[rows, :] = (y * (1.0 + sc2) + sh2).astype(BF16)

    for j in range(D_FF // nb):
        cols = slice(j * nb, (j + 1) * nb)
        g_ref[HIST:HIST + ts, cols] = jnp.dot(hb_ref[...], wup_ref[:, cols],
                                              preferred_element_type=F32)
        u = jnp.dot(hb_ref[...], wup_ref[:, D_FF + j * nb:D_FF + (j + 1) * nb],
                    preferred_element_type=F32)
        gc = _causal_conv(g_ref, HIST, ts, cols, fcw_ref, cols, fcb_ref[:, cols], FFN_CONV)
        a_ref[:, cols] = (_gelu(gc) * u).astype(BF16)

    g_ref[0:HIST, :] = g_ref[ts:ts + HIST, :]

    f = jnp.dot(a_ref[...], wdn_ref[...], preferred_element_type=F32)
    for c in range(nr):
        rows = slice(c * CHUNK, (c + 1) * CHUNK)
        x2 = x_ref[0, rows, :] + g2 * f[rows, :]
        y = x2 * _rms_scale(x2) * fg_ref[...]
        o_ref[0, rows, :] = y * (1.0 + fsc) + fsh


def _channel_mix(x, mod, fmod, n2g, w_up, fcw, fcb, w_down, fg, ts, nb):
    bsz, seq, _ = x.shape
    const2 = lambda b, t: (0, 0)
    return pl.pallas_call(
        functools.partial(_ffn_kernel, ts=ts, nb=nb),
        grid=(bsz, seq // ts),
        in_specs=[
            pl.BlockSpec((1, ts, D_MODEL), lambda b, t: (b, t, 0)),
            pl.BlockSpec((1, 6, D_MODEL), lambda b, t: (b, 0, 0)),
            pl.BlockSpec((1, 2, D_MODEL), lambda b, t: (b, 0, 0)),
            pl.BlockSpec((1, D_MODEL), const2),
            pl.BlockSpec((D_MODEL, 2 * D_FF), const2),
            pl.BlockSpec((FFN_CONV, D_FF), const2),
            pl.BlockSpec((1, D_FF), const2),
            pl.BlockSpec((D_FF, D_MODEL), const2),
            pl.BlockSpec((1, D_MODEL), const2),
        ],
        out_specs=pl.BlockSpec((1, ts, D_MODEL), lambda b, t: (b, t, 0)),
        out_shape=jax.ShapeDtypeStruct(x.shape, F32),
        scratch_shapes=[
            pltpu.VMEM((ts, D_MODEL), BF16),
            pltpu.VMEM((HIST + ts, D_FF), F32),
            pltpu.VMEM((ts, D_FF), BF16),
        ],
        compiler_params=pltpu.CompilerParams(
            dimension_semantics=("arbitrary", "arbitrary"),
            vmem_limit_bytes=VMEM_LIMIT_BYTES),
        name="channel_mix",
    )(x, mod, fmod, n2g, w_up, fcw, fcb, w_down, fg)


def kernel(x, c, ada_w, ada_b, norm1_g, w_in, gm_vnorm_g, gm_spatial_w, gm_spatial_b, ml_conv_w,
           ml_conv_b, ml_i_b, ml_f_b, ml_hnorm_g, w_out, norm2_g, ffn_w_up, ffn_conv_w, ffn_conv_b,
           ffn_w_down, final_ada_w, final_ada_b, final_g):
    depth = ada_w.shape[0]
    assert depth == 1, "the final norm is fused into the channel-mix call of a single layer"
    bsz = x.shape[0]
    fmod = _modulation(c, final_ada_w, final_ada_b[None, :]).reshape(bsz, 2, D_MODEL)
    for l in range(depth):
        mod = _modulation(c, ada_w[l], ada_b[l][None, :]).reshape(bsz, 6, D_MODEL)

        w_main = w_in[l][:, :N_MAIN].astype(BF16)
        w_gate_t = jnp.zeros((16, D_MODEL), F32).at[0:8].set(w_in[l][:, N_MAIN:].T).astype(BF16)
        gate_b = jnp.broadcast_to(jnp.concatenate([ml_i_b[l], ml_f_b[l]])[:, None], (SUBLANES, 128))
        bs_col = jnp.zeros((CHUNK, 128), F32).at[:, 0:N_HEADS].set(gm_spatial_b[l].T)
        x = _token_mix(x, mod, norm1_g[l][None, :], w_main, w_gate_t, gate_b,
                       gm_vnorm_g[l][None, :], gm_spatial_w[l], bs_col, ml_conv_w[l],
                       ml_conv_b[l][None, :], ml_hnorm_g[l][None, :], w_out[l].astype(BF16), ts=512)
        x = _channel_mix(x, mod, fmod, norm2_g[l][None, :], ffn_w_up[l].astype(BF16), ffn_conv_w[l],
                         ffn_conv_b[l][None, :], ffn_w_down[l].astype(BF16), final_g[None, :],
                         ts=512, nb=256)
    return x
```

```python
import functools
import math

import jax
import jax.numpy as jnp
from jax import lax
from jax.experimental import pallas as pl
from jax.experimental.pallas import tpu as pltpu

D_MODEL = 1024
CHUNK = 128
HEAD_DIM = 128
N_HEADS = 4
GM_WIDTH = N_HEADS * HEAD_DIM
ML_WIDTH = N_HEADS * HEAD_DIM
ML_CONV = 4
FFN_CONV = 3
D_FF = 2816
EPS = 1e-6
K_SCALE = HEAD_DIM ** -0.5
LOG_K_SCALE = -0.5 * math.log(HEAD_DIM)
N_MAIN = 2 * GM_WIDTH + 4 * ML_WIDTH
QK_BASE = 2 * GM_WIDTH
SUBLANES = 8
HIST = SUBLANES
VMEM_LIMIT_BYTES = 56 * 1024 * 1024

F32 = jnp.float32
BF16 = jnp.bfloat16


def _gelu(x):
    return 0.5 * x * (1.0 + lax.erf(x * 0.7071067811865476))


def _silu(x):
    return x * jax.nn.sigmoid(x)


def _log_sigmoid(x):
    return jnp.minimum(x, 0.0) - jnp.log1p(jnp.exp(-jnp.abs(x)))


def _rms_scale(x):
    return lax.rsqrt(jnp.mean(x * x, axis=-1, keepdims=True) + EPS)


def _causal_conv(ref, row0, nrows, cols, w_ref, wcols, bias, width):
    acc = ref[row0:row0 + nrows, cols] * w_ref[width - 1:width, wcols] + bias
    for d in range(1, width):
        acc = acc + ref[row0 - d:row0 - d + nrows, cols] * w_ref[width - 1 - d:width - d, wcols]
    return acc


def _mod_kernel(c_ref, w_ref, b_ref, o_ref):
    ca = _silu(c_ref[...]).astype(BF16)
    o_ref[...] = jnp.dot(ca, w_ref[...].astype(BF16), preferred_element_type=F32) + b_ref[...]


def _modulation(c, w, b):
    bsz = c.shape[0]
    n = w.shape[1]
    bn = D_MODEL
    return pl.pallas_call(
        _mod_kernel,
        grid=(n // bn,),
        in_specs=[
            pl.BlockSpec((bsz, D_MODEL), lambda j: (0, 0)),
            pl.BlockSpec((D_MODEL, bn), lambda j: (0, j)),
            pl.BlockSpec((1, bn), lambda j: (0, j)),
        ],
        out_specs=pl.BlockSpec((bsz, bn), lambda j: (0, j)),
        out_shape=jax.ShapeDtypeStruct((bsz, n), F32),
        compiler_params=pltpu.CompilerParams(dimension_semantics=("arbitrary",)),
        name="adaln_modulation",
    )(c, w, b)


PROJ_BLOCK = 256


def _mix_kernel(xa_ref, xc_ref, moda_ref, modc_ref, n1g_ref, win_ref, wgt_ref, gb_ref, vng_ref,
                ws_ref, bs_ref, cw_ref, cb_ref, hng_ref, wout_ref, o_ref,
                hb_ref, proj0_ref, proj1_ref, gr0_ref, gr1_ref, ymix0_ref, ymix1_ref, wsb_ref,
                mst_ref, m_ref, *, ts, nt, n_tiles):
    nc = ts // CHUNK
    s = pl.program_id(0)
    t_a = lax.rem(jnp.minimum(s, n_tiles - 1), nt)
    t_b = lax.rem(jnp.clip(s - 1, 0, n_tiles - 1), nt)
    qk_cols = slice(QK_BASE, QK_BASE + 2 * ML_WIDTH)

    tri_r = lax.broadcasted_iota(jnp.int32, (CHUNK, CHUNK), 0)
    tri_c = lax.broadcasted_iota(jnp.int32, (CHUNK, CHUNK), 1)
    tril = tri_c <= tri_r

    @pl.when(s == 0)
    def _():
        proj1_ref[...] = jnp.zeros_like(proj1_ref)
        gr1_ref[...] = jnp.zeros_like(gr1_ref)
        ymix1_ref[...] = jnp.zeros_like(ymix1_ref)
        for h in range(N_HEADS):
            wsb_ref[h] = jnp.where(tril, ws_ref[h], 0.0).astype(BF16)

    @pl.when(t_b == 0)
    def _():
        mst_ref[...] = jnp.zeros_like(mst_ref)
        m_ref[...] = jnp.zeros_like(m_ref)

    def body(proj_w, proj_r, gr_w, gr_r, ymix_w, ymix_r):
        moda = moda_ref[0]
        sh1, sc1 = moda[0:1], moda[1:2]
        gain1 = n1g_ref[...] * (1.0 + sc1)
        for c in range(nc):
            rows = slice(c * CHUNK, (c + 1) * CHUNK)
            xs = xa_ref[0, rows, :]
            hb_ref[rows, :] = (xs * _rms_scale(xs) * gain1 + sh1).astype(BF16)

        gates = lax.dot_general(wgt_ref[...], hb_ref[...], (((1,), (1,)), ((), ())),
                                preferred_element_type=F32)
        gates = gates[0:SUBLANES] + gb_ref[:, 0:1]
        logf = _log_sigmoid(pltpu.roll(gates, N_HEADS, axis=0))
        lane = lax.broadcasted_iota(jnp.int32, (SUBLANES, ts), 1) % CHUNK
        bsum = logf
        for d in (1, 2, 4, 8, 16, 32, 64):
            bsum = bsum + jnp.where(lane >= d, pltpu.roll(bsum, d, axis=1), 0.0)
        rr = gates - bsum
        crun = rr
        for d in (1, 2, 4, 8, 16, 32, 64):
            crun = jnp.maximum(crun, jnp.where(lane >= d, pltpu.roll(crun, d, axis=1), -jnp.inf))
        gr_w[0] = gates
        gr_w[1] = bsum
        gr_w[2] = rr
        gr_w[3] = crun

        hist = proj_r[ts:ts + HIST, qk_cols]
        proj_w[0:HIST, qk_cols] = jnp.where(t_a == 0, jnp.zeros_like(hist), hist)

        g1 = modc_ref[0][2:3]
        ii, b, r, cr = gr_r[0], gr_r[1], gr_r[2], gr_r[3]
        zpad = jnp.zeros((CHUNK - 3 * SUBLANES, CHUNK), F32)
        m = m_ref[:, 0:1]
        zts, wgts, decays = [], [], []
        for c in range(nc):
            rows = slice(c * CHUNK, (c + 1) * CHUNK)
            b_c, cr_c, ii_c = b[:, rows], cr[:, rows], ii[:, rows]
            b_last = b_c[:, CHUNK - 1:CHUNK]
            a_c = b_last - b_c + ii_c
            a_max = jnp.max(a_c, axis=1, keepdims=True)
            m_new = jnp.maximum(b_last + m, a_max)
            decays.append(jnp.exp(b_last + m - m_new))
            wgts.append(jnp.exp(a_c - a_max) * (jnp.exp(a_max - m_new) * K_SCALE))
            mm = jnp.maximum(m, cr_c)
            zts.append(jnp.concatenate([mm, jnp.exp(m - mm), jnp.exp(-(b_c + mm)), zpad], axis=0).T)
            m = m_new
        m_ref[...] = jnp.broadcast_to(m, m_ref.shape)

        ones_col = jnp.where(tri_c == 0, 1.0, 0.0).astype(BF16)

        n_in = N_MAIN // PROJ_BLOCK
        n_blocks = n_in + D_MODEL // PROJ_BLOCK
        n_items = nc * N_HEADS
        order = []
        for grp in range(D_MODEL // PROJ_BLOCK):
            order.append(n_in + grp)
            order.extend(range(grp * 3, grp * 3 + 3))
        assert sorted(order) == list(range(n_blocks))
        item = 0

        def issue_block(j):
            if j < n_in:
                pc = slice(j * PROJ_BLOCK, (j + 1) * PROJ_BLOCK)
                proj_w[HIST:HIST + ts, pc] = jnp.dot(hb_ref[...], win_ref[:, pc],
                                                     preferred_element_type=F32)
            else:
                oc = slice((j - n_in) * PROJ_BLOCK, (j - n_in + 1) * PROJ_BLOCK)
                y = jnp.dot(ymix_r[...], wout_ref[:, oc], preferred_element_type=F32)
                o_ref[0, :, oc] = xc_ref[0, :, oc] + g1[:, oc] * y

        for c in range(nc):
            rows = slice(c * CHUNK, (c + 1) * CHUNK)
            prow = slice(HIST + c * CHUNK, HIST + (c + 1) * CHUNK)
            zt, wgt, decay = zts[c], wgts[c], decays[c]
            r_c = r[:, rows] + LOG_K_SCALE

            for h in range(N_HEADS):
                blk_lo = (item * n_blocks + n_items - 1) // n_items
                blk_hi = ((item + 1) * n_blocks + n_items - 1) // n_items
                for j in range(blk_lo, blk_hi):
                    issue_block(order[j])
                item += 1

                hc = slice(h * HEAD_DIM, (h + 1) * HEAD_DIM)
                u = _gelu(proj_r[prow, hc])
                v = _gelu(proj_r[prow, GM_WIDTH + h * HEAD_DIM:GM_WIDTH + (h + 1) * HEAD_DIM])
                v = v * _rms_scale(v) * vng_ref[:, hc]
                mixed = (jnp.dot(wsb_ref[h], v.astype(BF16), preferred_element_type=F32)
                         + bs_ref[:, h:h + 1])
                ymix_w[rows, hc] = (u * mixed).astype(BF16)

                qc = slice(h * HEAD_DIM, (h + 1) * HEAD_DIM)
                kc = slice(ML_WIDTH + h * HEAD_DIM, ML_WIDTH + (h + 1) * HEAD_DIM)
                q = _silu(_causal_conv(proj_r, prow.start, CHUNK,
                                       slice(QK_BASE + qc.start, QK_BASE + qc.stop),
                                       cw_ref, qc, cb_ref[:, qc], ML_CONV))
                k = _silu(_causal_conv(proj_r, prow.start, CHUNK,
                                       slice(QK_BASE + kc.start, QK_BASE + kc.stop),
                                       cw_ref, kc, cb_ref[:, kc], ML_CONV))
                vbase = QK_BASE + 2 * ML_WIDTH + h * HEAD_DIM
                obase = QK_BASE + 3 * ML_WIDTH + h * HEAD_DIM
                vv = proj_r[prow, vbase:vbase + HEAD_DIM]
                og = proj_r[prow, obase:obase + HEAD_DIM]

                kt = k.T
                sc = jnp.dot(q.astype(BF16), kt.astype(BF16), preferred_element_type=F32)
                mm = zt[:, h:h + 1]
                iw = zt[:, SUBLANES + h:SUBLANES + h + 1]
                en = zt[:, 2 * SUBLANES + h:2 * SUBLANES + h + 1]
                p = jnp.exp(jnp.where(tril, r_c[h:h + 1, :] - mm, -jnp.inf))
                lhs = jnp.concatenate([(sc * p).astype(BF16), (iw * q).astype(BF16)], axis=1)
                v_aug = jnp.concatenate([vv.astype(BF16), ones_col], axis=1)
                st_prev = mst_ref[h]
                rhs = jnp.concatenate([v_aug, st_prev.astype(BF16)], axis=0)
                nd = jnp.dot(lhs, rhs, preferred_element_type=F32)
                den = nd[:, HEAD_DIM:HEAD_DIM + 1]
                hh = nd[:, 0:HEAD_DIM] / jnp.maximum(jnp.abs(den), en)
                hn = hh * _rms_scale(hh) * hng_ref[:, hc]
                ymix_w[rows, GM_WIDTH + h * HEAD_DIM:GM_WIDTH + (h + 1) * HEAD_DIM] = (
                    jax.nn.sigmoid(og) * hn).astype(BF16)

                kw = (kt * wgt[h:h + 1, :]).astype(BF16)
                st_c = jnp.dot(kw, v_aug, preferred_element_type=F32)
                mst_ref[h] = decay[h:h + 1, :] * st_prev + st_c

    parity = lax.rem(s, 2)

    @pl.when(parity == 0)
    def _():
        body(proj0_ref, proj1_ref, gr0_ref, gr1_ref, ymix0_ref, ymix1_ref)

    @pl.when(parity == 1)
    def _():
        body(proj1_ref, proj0_ref, gr1_ref, gr0_ref, ymix1_ref, ymix0_ref)


def _token_mix(x, mod, n1g, w_main, w_gate_t, gate_b, vng, ws, bs_col, cw, cb, hng, w_out, ts):
    bsz, seq, _ = x.shape
    nt = seq // ts
    n_tiles = bsz * nt
    const2 = lambda s: (0, 0)

    def tile_a(s):
        j = jnp.minimum(s, n_tiles - 1)
        return (j // nt, j % nt, 0)

    def tile_c(s):
        j = jnp.maximum(s - 2, 0)
        return (j // nt, j % nt, 0)

    return pl.pallas_call(
        functools.partial(_mix_kernel, ts=ts, nt=nt, n_tiles=n_tiles),
        grid=(n_tiles + 2,),
        in_specs=[
            pl.BlockSpec((1, ts, D_MODEL), tile_a),
            pl.BlockSpec((1, ts, D_MODEL), tile_c),
            pl.BlockSpec((1, 6, D_MODEL), lambda s: (tile_a(s)[0], 0, 0)),
            pl.BlockSpec((1, 6, D_MODEL), lambda s: (tile_c(s)[0], 0, 0)),
            pl.BlockSpec((1, D_MODEL), const2),
            pl.BlockSpec((D_MODEL, N_MAIN), const2),
            pl.BlockSpec((16, D_MODEL), const2),
            pl.BlockSpec((SUBLANES, 128), const2),
            pl.BlockSpec((1, GM_WIDTH), const2),
            pl.BlockSpec((N_HEADS, CHUNK, CHUNK), lambda s: (0, 0, 0)),
            pl.BlockSpec((CHUNK, 128), const2),
            pl.BlockSpec((ML_CONV, 2 * ML_WIDTH), const2),
            pl.BlockSpec((1, 2 * ML_WIDTH), const2),
            pl.BlockSpec((1, ML_WIDTH), const2),
            pl.BlockSpec((D_MODEL, D_MODEL), const2),
        ],
        out_specs=pl.BlockSpec((1, ts, D_MODEL), tile_c),
        out_shape=jax.ShapeDtypeStruct(x.shape, F32),
        scratch_shapes=[
            pltpu.VMEM((ts, D_MODEL), BF16),
            pltpu.VMEM((HIST + ts, N_MAIN), F32),
            pltpu.VMEM((HIST + ts, N_MAIN), F32),
            pltpu.VMEM((4, SUBLANES, ts), F32),
            pltpu.VMEM((4, SUBLANES, ts), F32),
            pltpu.VMEM((ts, D_MODEL), BF16),
            pltpu.VMEM((ts, D_MODEL), BF16),
            pltpu.VMEM((N_HEADS, CHUNK, CHUNK), BF16),
            pltpu.VMEM((N_HEADS, HEAD_DIM, 2 * HEAD_DIM), F32),
            pltpu.VMEM((SUBLANES, 128), F32),
        ],
        compiler_params=pltpu.CompilerParams(
            dimension_semantics=("arbitrary",),
            vmem_limit_bytes=VMEM_LIMIT_BYTES),
        name="token_mix",
    )(x, x, mod, mod, n1g, w_main, w_gate_t, gate_b, vng, ws, bs_col, cw, cb, hng, w_out)


def _ffn_kernel(x_ref, mod_ref, fmod_ref, n2g_ref, wup_ref, fcw_ref, fcb_ref, wdn_ref, fg_ref, o_ref,
                hb_ref, g_ref, a_ref, *, ts, nb):
    @pl.when(pl.program_id(1) == 0)
    def _():
        g_ref[0:HIST, :] = jnp.zeros((HIST, D_FF), F32)

    mod = mod_ref[0]
    sh2, sc2, g2 = mod[3:4], mod[4:5], mod[5:6]
    fmod = fmod_ref[0]
    fsh, fsc = fmod[0:1], fmod[1:2]
    gain2 = n2g_ref[...] * (1.0 + sc2)
    gainf = fg_ref[...] * (1.0 + fsc)
    nr = ts // CHUNK

    for c in range(nr):
        rows = slice(c * CHUNK, (c + 1) * CHUNK)
        xs = x_ref[0, rows, :]
        hb_ref[rows, :] = (xs * _rms_scale(xs) * gain2 + sh2).astype(BF16)

    for j in range(D_FF // nb):
        cols = slice(j * nb, (j + 1) * nb)
        g_ref[HIST:HIST + ts, cols] = jnp.dot(hb_ref[...], wup_ref[:, cols],
                                              preferred_element_type=F32)
        u = jnp.dot(hb_ref[...], wup_ref[:, D_FF + j * nb:D_FF + (j + 1) * nb],
                    preferred_element_type=F32)
        gc = _causal_conv(g_ref, HIST, ts, cols, fcw_ref, cols, fcb_ref[:, cols], FFN_CONV)
        a_ref[:, cols] = (_gelu(gc) * u).astype(BF16)

    g_ref[0:HIST, :] = g_ref[ts:ts + HIST, :]

    f = jnp.dot(a_ref[...], wdn_ref[...], preferred_element_type=F32)
    for c in range(nr):
        rows = slice(c * CHUNK, (c + 1) * CHUNK)
        x2 = x_ref[0, rows, :] + g2 * f[rows, :]
        o_ref[0, rows, :] = x2 * _rms_scale(x2) * gainf + fsh


def _channel_mix(x, mod, fmod, n2g, w_up, fcw, fcb, w_down, fg, ts, nb):
    bsz, seq, _ = x.shape
    const2 = lambda b, t: (0, 0)
    resident = pl.Buffered(1)
    return pl.pallas_call(
        functools.partial(_ffn_kernel, ts=ts, nb=nb),
        grid=(bsz, seq // ts),
        in_specs=[
            pl.BlockSpec((1, ts, D_MODEL), lambda b, t: (b, t, 0)),
            pl.BlockSpec((1, 6, D_MODEL), lambda b, t: (b, 0, 0)),
            pl.BlockSpec((1, 2, D_MODEL), lambda b, t: (b, 0, 0)),
            pl.BlockSpec((1, D_MODEL), const2),
            pl.BlockSpec((D_MODEL, 2 * D_FF), const2, pipeline_mode=resident),
            pl.BlockSpec((FFN_CONV, D_FF), const2),
            pl.BlockSpec((1, D_FF), const2),
            pl.BlockSpec((D_FF, D_MODEL), const2, pipeline_mode=resident),
            pl.BlockSpec((1, D_MODEL), const2),
        ],
        out_specs=pl.BlockSpec((1, ts, D_MODEL), lambda b, t: (b, t, 0)),
        out_shape=jax.ShapeDtypeStruct(x.shape, F32),
        scratch_shapes=[
            pltpu.VMEM((ts, D_MODEL), BF16),
            pltpu.VMEM((HIST + ts, D_FF), F32),
            pltpu.VMEM((ts, D_FF), BF16),
        ],
        compiler_params=pltpu.CompilerParams(
            dimension_semantics=("arbitrary", "arbitrary"),
            vmem_limit_bytes=VMEM_LIMIT_BYTES),
        name="channel_mix",
    )(x, mod, fmod, n2g, w_up, fcw, fcb, w_down, fg)


def kernel(x, c, ada_w, ada_b, norm1_g, w_in, gm_vnorm_g, gm_spatial_w, gm_spatial_b, ml_conv_w,
           ml_conv_b, ml_i_b, ml_f_b, ml_hnorm_g, w_out, norm2_g, ffn_w_up, ffn_conv_w, ffn_conv_b,
           ffn_w_down, final_ada_w, final_ada_b, final_g):
    depth = ada_w.shape[0]
    assert depth == 1, "the final norm is fused into the channel-mix call of a single layer"
    bsz = x.shape[0]
    fmod = _modulation(c, final_ada_w, final_ada_b[None, :]).reshape(bsz, 2, D_MODEL)
    for l in range(depth):
        mod = _modulation(c, ada_w[l], ada_b[l][None, :]).reshape(bsz, 6, D_MODEL)

        w_main = w_in[l].astype(BF16)
        w_gate_t = jnp.zeros((16, D_MODEL), F32).at[0:8].set(w_in[l][:, N_MAIN:].T).astype(BF16)
        gate_b = jnp.broadcast_to(jnp.concatenate([ml_i_b[l], ml_f_b[l]])[:, None], (SUBLANES, 128))
        bs_col = jnp.zeros((CHUNK, 128), F32).at[:, 0:N_HEADS].set(gm_spatial_b[l].T)
        x = _token_mix(x, mod, norm1_g[l][None, :], w_main, w_gate_t, gate_b,
                       gm_vnorm_g[l][None, :], gm_spatial_w[l], bs_col, ml_conv_w[l],
                       ml_conv_b[l][None, :], ml_hnorm_g[l][None, :], w_out[l].astype(BF16), ts=512)
        x = _channel_mix(x, mod, fmod, norm2_g[l][None, :], ffn_w_up[l].astype(BF16), ffn_conv_w[l],
                         ffn_conv_b[l][None, :], ffn_w_down[l].astype(BF16), final_g[None, :],
                         ts=512, nb=256)
    return x
```

```python
import functools
import math

import jax
import jax.numpy as jnp
from jax import lax
from jax.experimental import pallas as pl
from jax.experimental.pallas import tpu as pltpu

D_MODEL = 1024
CHUNK = 128
HEAD_DIM = 128
N_HEADS = 4
GM_WIDTH = N_HEADS * HEAD_DIM
ML_WIDTH = N_HEADS * HEAD_DIM
ML_CONV = 4
FFN_CONV = 3
D_FF = 2816
EPS = 1e-6
K_SCALE = HEAD_DIM ** -0.5
LOG_K_SCALE = -0.5 * math.log(HEAD_DIM)
N_MAIN = 2 * GM_WIDTH + 4 * ML_WIDTH
QK_BASE = 2 * GM_WIDTH
SUBLANES = 8
HIST = SUBLANES
VMEM_LIMIT_BYTES = 56 * 1024 * 1024

F32 = jnp.float32
BF16 = jnp.bfloat16


def _gelu(x):
    return 0.5 * x * (1.0 + lax.erf(x * 0.7071067811865476))


def _silu(x):
    return x * jax.nn.sigmoid(x)


def _log_sigmoid(x):
    return jnp.minimum(x, 0.0) - jnp.log1p(jnp.exp(-jnp.abs(x)))


def _rms_scale(x):
    return lax.rsqrt(jnp.mean(x * x, axis=-1, keepdims=True) + EPS)


def _causal_conv(ref, row0, nrows, cols, w_ref, wcols, bias, width):
    acc = ref[row0:row0 + nrows, cols] * w_ref[width - 1:width, wcols] + bias
    for d in range(1, width):
        acc = acc + ref[row0 - d:row0 - d + nrows, cols] * w_ref[width - 1 - d:width - d, wcols]
    return acc


def _mod_kernel(c_ref, w_ref, b_ref, o_ref):
    ca = _silu(c_ref[...]).astype(BF16)
    o_ref[...] = jnp.dot(ca, w_ref[...].astype(BF16), preferred_element_type=F32) + b_ref[...]


def _modulation(c, w, b):
    bsz = c.shape[0]
    n = w.shape[1]
    bn = D_MODEL
    return pl.pallas_call(
        _mod_kernel,
        grid=(n // bn,),
        in_specs=[
            pl.BlockSpec((bsz, D_MODEL), lambda j: (0, 0)),
            pl.BlockSpec((D_MODEL, bn), lambda j: (0, j)),
            pl.BlockSpec((1, bn), lambda j: (0, j)),
        ],
        out_specs=pl.BlockSpec((bsz, bn), lambda j: (0, j)),
        out_shape=jax.ShapeDtypeStruct((bsz, n), F32),
        compiler_params=pltpu.CompilerParams(dimension_semantics=("arbitrary",)),
        name="adaln_modulation",
    )(c, w, b)


PROJ_BLOCK = 256


def _mix_kernel(xa_ref, xc_ref, moda_ref, modc_ref, n1g_ref, win_ref, wgt_ref, gb_ref, vng_ref,
                ws_ref, bs_ref, cw_ref, cb_ref, hng_ref, wout_ref, o_ref,
                hb_ref, proj0_ref, proj1_ref, gr0_ref, gr1_ref, ymix0_ref, ymix1_ref, wsb_ref,
                mst_ref, m_ref, *, ts, nt, n_tiles):
    nc = ts // CHUNK
    s = pl.program_id(0)
    t_a = lax.rem(jnp.minimum(s, n_tiles - 1), nt)
    t_b = lax.rem(jnp.clip(s - 1, 0, n_tiles - 1), nt)
    qk_cols = slice(QK_BASE, QK_BASE + 2 * ML_WIDTH)

    tri_r = lax.broadcasted_iota(jnp.int32, (CHUNK, CHUNK), 0)
    tri_c = lax.broadcasted_iota(jnp.int32, (CHUNK, CHUNK), 1)
    tril = tri_c <= tri_r

    @pl.when(s == 0)
    def _():
        proj1_ref[...] = jnp.zeros_like(proj1_ref)
        gr1_ref[...] = jnp.zeros_like(gr1_ref)
        ymix1_ref[...] = jnp.zeros_like(ymix1_ref)
        for h in range(N_HEADS):
            wsb_ref[h] = jnp.where(tril, ws_ref[h], 0.0).astype(BF16)

    @pl.when(t_b == 0)
    def _():
        mst_ref[...] = jnp.zeros_like(mst_ref)
        m_ref[...] = jnp.zeros_like(m_ref)

    def body(proj_w, proj_r, gr_w, gr_r, ymix_w, ymix_r):
        moda = moda_ref[0]
        sh1, sc1 = moda[0:1], moda[1:2]
        gain1 = n1g_ref[...] * (1.0 + sc1)
        for c in range(nc):
            rows = slice(c * CHUNK, (c + 1) * CHUNK)
            xs = xa_ref[0, rows, :]
            hb_ref[rows, :] = (xs * _rms_scale(xs) * gain1 + sh1).astype(BF16)

        gates = lax.dot_general(wgt_ref[...], hb_ref[...], (((1,), (1,)), ((), ())),
                                preferred_element_type=F32)
        gates = gates[0:SUBLANES] + gb_ref[:, 0:1]
        logf = _log_sigmoid(pltpu.roll(gates, N_HEADS, axis=0))
        lane = lax.broadcasted_iota(jnp.int32, (SUBLANES, ts), 1) % CHUNK
        bsum = logf
        for d in (1, 2, 4, 8, 16, 32, 64):
            bsum = bsum + jnp.where(lane >= d, pltpu.roll(bsum, d, axis=1), 0.0)
        rr = gates - bsum
        crun = rr
        for d in (1, 2, 4, 8, 16, 32, 64):
            crun = jnp.maximum(crun, jnp.where(lane >= d, pltpu.roll(crun, d, axis=1), -jnp.inf))
        gr_w[0] = gates
        gr_w[1] = bsum
        gr_w[2] = rr
        gr_w[3] = crun

        hist = proj_r[ts:ts + HIST, qk_cols]
        proj_w[0:HIST, qk_cols] = jnp.where(t_a == 0, jnp.zeros_like(hist), hist)

        g1 = modc_ref[0][2:3]
        ii, b, r, cr = gr_r[0], gr_r[1], gr_r[2], gr_r[3]
        zpad = jnp.zeros((CHUNK - 3 * SUBLANES, CHUNK), F32)
        m = m_ref[:, 0:1]
        zts, wgts, decays = [], [], []
        for c in range(nc):
            rows = slice(c * CHUNK, (c + 1) * CHUNK)
            b_c, cr_c, ii_c = b[:, rows], cr[:, rows], ii[:, rows]
            b_last = b_c[:, CHUNK - 1:CHUNK]
            a_c = b_last - b_c + ii_c
            a_max = jnp.max(a_c, axis=1, keepdims=True)
            m_new = jnp.maximum(b_last + m, a_max)
            decays.append(jnp.exp(b_last + m - m_new))
            wgts.append(jnp.exp(a_c - a_max) * (jnp.exp(a_max - m_new) * K_SCALE))
            mm = jnp.maximum(m, cr_c)
            zts.append(jnp.concatenate([mm, jnp.exp(m - mm), jnp.exp(-(b_c + mm)), zpad], axis=0).T)
            m = m_new
        m_ref[...] = jnp.broadcast_to(m, m_ref.shape)

        ones_col = jnp.where(tri_c == 0, 1.0, 0.0).astype(BF16)

        n_in = N_MAIN // PROJ_BLOCK
        n_blocks = n_in + D_MODEL // PROJ_BLOCK
        n_items = nc * N_HEADS
        order = []
        for grp in range(D_MODEL // PROJ_BLOCK):
            order.append(n_in + grp)
            order.extend(range(grp * 3, grp * 3 + 3))
        assert sorted(order) == list(range(n_blocks))
        item = 0

        def issue_block(j):
            if j < n_in:
                pc = slice(j * PROJ_BLOCK, (j + 1) * PROJ_BLOCK)
                proj_w[HIST:HIST + ts, pc] = jnp.dot(hb_ref[...], win_ref[:, pc],
                                                     preferred_element_type=F32)
            else:
                oc = slice((j - n_in) * PROJ_BLOCK, (j - n_in + 1) * PROJ_BLOCK)
                y = jnp.dot(ymix_r[...], wout_ref[:, oc], preferred_element_type=F32)
                o_ref[0, :, oc] = xc_ref[0, :, oc] + g1[:, oc] * y

        for c in range(nc):
            rows = slice(c * CHUNK, (c + 1) * CHUNK)
            prow = slice(HIST + c * CHUNK, HIST + (c + 1) * CHUNK)
            zt, wgt, decay = zts[c], wgts[c], decays[c]
            r_c = r[:, rows] + LOG_K_SCALE

            for h in range(N_HEADS):
                blk_lo = (item * n_blocks + n_items - 1) // n_items
                blk_hi = ((item + 1) * n_blocks + n_items - 1) // n_items
                for j in range(blk_lo, blk_hi):
                    issue_block(order[j])
                item += 1

                hc = slice(h * HEAD_DIM, (h + 1) * HEAD_DIM)
                u = _gelu(proj_r[prow, hc])
                v = _gelu(proj_r[prow, GM_WIDTH + h * HEAD_DIM:GM_WIDTH + (h + 1) * HEAD_DIM])
                v = v * _rms_scale(v) * vng_ref[:, hc]
                mixed = (jnp.dot(wsb_ref[h], v.astype(BF16), preferred_element_type=F32)
                         + bs_ref[:, h:h + 1])
                ymix_w[rows, hc] = (u * mixed).astype(BF16)

                qc = slice(h * HEAD_DIM, (h + 1) * HEAD_DIM)
                kc = slice(ML_WIDTH + h * HEAD_DIM, ML_WIDTH + (h + 1) * HEAD_DIM)
                q = _silu(_causal_conv(proj_r, prow.start, CHUNK,
                                       slice(QK_BASE + qc.start, QK_BASE + qc.stop),
                                       cw_ref, qc, cb_ref[:, qc], ML_CONV))
                k = _silu(_causal_conv(proj_r, prow.start, CHUNK,
                                       slice(QK_BASE + kc.start, QK_BASE + kc.stop),
                                       cw_ref, kc, cb_ref[:, kc], ML_CONV))
                vbase = QK_BASE + 2 * ML_WIDTH + h * HEAD_DIM
                obase = QK_BASE + 3 * ML_WIDTH + h * HEAD_DIM
                vv = proj_r[prow, vbase:vbase + HEAD_DIM]
                og = proj_r[prow, obase:obase + HEAD_DIM]

                kt = k.T
                sc = jnp.dot(q.astype(BF16), kt.astype(BF16), preferred_element_type=F32)
                mm = zt[:, h:h + 1]
                iw = zt[:, SUBLANES + h:SUBLANES + h + 1]
                en = zt[:, 2 * SUBLANES + h:2 * SUBLANES + h + 1]
                p = jnp.exp(jnp.where(tril, r_c[h:h + 1, :] - mm, -jnp.inf))
                lhs = jnp.concatenate([(sc * p).astype(BF16), (iw * q).astype(BF16)], axis=1)
                v_aug = jnp.concatenate([vv.astype(BF16), ones_col], axis=1)
                st_prev = mst_ref[h]
                rhs = jnp.concatenate([v_aug, st_prev.astype(BF16)], axis=0)
                nd = jnp.dot(lhs, rhs, preferred_element_type=F32)
                den = nd[:, HEAD_DIM:HEAD_DIM + 1]
                hh = nd[:, 0:HEAD_DIM] / jnp.maximum(jnp.abs(den), en)
                hn = hh * _rms_scale(hh) * hng_ref[:, hc]
                ymix_w[rows, GM_WIDTH + h * HEAD_DIM:GM_WIDTH + (h + 1) * HEAD_DIM] = (
                    jax.nn.sigmoid(og) * hn).astype(BF16)

                kw = (kt * wgt[h:h + 1, :]).astype(BF16)
                st_c = jnp.dot(kw, v_aug, preferred_element_type=F32)
                mst_ref[h] = decay[h:h + 1, :] * st_prev + st_c

    parity = lax.rem(s, 2)

    @pl.when(parity == 0)
    def _():
        body(proj0_ref, proj1_ref, gr0_ref, gr1_ref, ymix0_ref, ymix1_ref)

    @pl.when(parity == 1)
    def _():
        body(proj1_ref, proj0_ref, gr1_ref, gr0_ref, ymix1_ref, ymix0_ref)


def _token_mix(x, mod, n1g, w_main, w_gate_t, gate_b, vng, ws, bs_col, cw, cb, hng, w_out, ts):
    bsz, seq, _ = x.shape
    nt = seq // ts
    n_tiles = bsz * nt
    const2 = lambda s: (0, 0)

    def tile_a(s):
        j = jnp.minimum(s, n_tiles - 1)
        return (j // nt, j % nt, 0)

    def tile_c(s):
        j = jnp.maximum(s - 2, 0)
        return (j // nt, j % nt, 0)

    return pl.pallas_call(
        functools.partial(_mix_kernel, ts=ts, nt=nt, n_tiles=n_tiles),
        grid=(n_tiles + 2,),
        in_specs=[
            pl.BlockSpec((1, ts, D_MODEL), tile_a),
            pl.BlockSpec((1, ts, D_MODEL), tile_c),
            pl.BlockSpec((1, 6, D_MODEL), lambda s: (tile_a(s)[0], 0, 0)),
            pl.BlockSpec((1, 6, D_MODEL), lambda s: (tile_c(s)[0], 0, 0)),
            pl.BlockSpec((1, D_MODEL), const2),
            pl.BlockSpec((D_MODEL, N_MAIN), const2),
            pl.BlockSpec((16, D_MODEL), const2),
            pl.BlockSpec((SUBLANES, 128), const2),
            pl.BlockSpec((1, GM_WIDTH), const2),
            pl.BlockSpec((N_HEADS, CHUNK, CHUNK), lambda s: (0, 0, 0)),
            pl.BlockSpec((CHUNK, 128), const2),
            pl.BlockSpec((ML_CONV, 2 * ML_WIDTH), const2),
            pl.BlockSpec((1, 2 * ML_WIDTH), const2),
            pl.BlockSpec((1, ML_WIDTH), const2),
            pl.BlockSpec((D_MODEL, D_MODEL), const2),
        ],
        out_specs=pl.BlockSpec((1, ts, D_MODEL), tile_c),
        out_shape=jax.ShapeDtypeStruct(x.shape, F32),
        scratch_shapes=[
            pltpu.VMEM((ts, D_MODEL), BF16),
            pltpu.VMEM((HIST + ts, N_MAIN), F32),
            pltpu.VMEM((HIST + ts, N_MAIN), F32),
            pltpu.VMEM((4, SUBLANES, ts), F32),
            pltpu.VMEM((4, SUBLANES, ts), F32),
            pltpu.VMEM((ts, D_MODEL), BF16),
            pltpu.VMEM((ts, D_MODEL), BF16),
            pltpu.VMEM((N_HEADS, CHUNK, CHUNK), BF16),
            pltpu.VMEM((N_HEADS, HEAD_DIM, 2 * HEAD_DIM), F32),
            pltpu.VMEM((SUBLANES, 128), F32),
        ],
        compiler_params=pltpu.CompilerParams(
            dimension_semantics=("arbitrary",),
            vmem_limit_bytes=VMEM_LIMIT_BYTES),
        name="token_mix",
    )(x, x, mod, mod, n1g, w_main, w_gate_t, gate_b, vng, ws, bs_col, cw, cb, hng, w_out)


def _ffn_kernel(x_ref, mod_ref, fmod_ref, n2g_ref, wup_ref, fcw_ref, fcb_ref, wdn_ref, fg_ref, o_ref,
                hb_ref, g_ref, a_ref, *, ts, nb):
    @pl.when(pl.program_id(1) == 0)
    def _():
        g_ref[0:HIST, :] = jnp.zeros((HIST, D_FF), F32)

    mod = mod_ref[0]
    sh2, sc2, g2 = mod[3:4], mod[4:5], mod[5:6]
    fmod = fmod_ref[0]
    fsh, fsc = fmod[0:1], fmod[1:2]
    gain2 = n2g_ref[...] * (1.0 + sc2)
    gainf = fg_ref[...] * (1.0 + fsc)
    th = ts // 2

    def norm(r0):
        for c in range(th // CHUNK):
            rows = slice(r0 + c * CHUNK, r0 + (c + 1) * CHUNK)
            xs = x_ref[0, rows, :]
            hb_ref[rows, :] = (xs * _rms_scale(xs) * gain2 + sh2).astype(BF16)

    def up(r0):
        rows = slice(r0, r0 + th)
        for j in range(D_FF // nb):
            cols = slice(j * nb, (j + 1) * nb)
            g_ref[HIST + r0:HIST + r0 + th, cols] = jnp.dot(hb_ref[rows, :], wup_ref[:, cols],
                                                            preferred_element_type=F32)
            u = jnp.dot(hb_ref[rows, :], wup_ref[:, D_FF + j * nb:D_FF + (j + 1) * nb],
                        preferred_element_type=F32)
            gc = _causal_conv(g_ref, HIST + r0, th, cols, fcw_ref, cols, fcb_ref[:, cols], FFN_CONV)
            a_ref[rows, cols] = (_gelu(gc) * u).astype(BF16)

    def down(r0):
        f = jnp.dot(a_ref[r0:r0 + th, :], wdn_ref[...], preferred_element_type=F32)
        for c in range(th // CHUNK):
            rows = slice(r0 + c * CHUNK, r0 + (c + 1) * CHUNK)
            x2 = x_ref[0, rows, :] + g2 * f[c * CHUNK:(c + 1) * CHUNK, :]
            o_ref[0, rows, :] = x2 * _rms_scale(x2) * gainf + fsh

    norm(0)
    norm(th)
    up(0)
    up(th)
    g_ref[0:HIST, :] = g_ref[ts:ts + HIST, :]
    down(0)
    down(th)


def _channel_mix(x, mod, fmod, n2g, w_up, fcw, fcb, w_down, fg, ts, nb):
    bsz, seq, _ = x.shape
    const2 = lambda b, t: (0, 0)
    resident = pl.Buffered(1)
    return pl.pallas_call(
        functools.partial(_ffn_kernel, ts=ts, nb=nb),
        grid=(bsz, seq // ts),
        in_specs=[
            pl.BlockSpec((1, ts, D_MODEL), lambda b, t: (b, t, 0)),
            pl.BlockSpec((1, 6, D_MODEL), lambda b, t: (b, 0, 0)),
            pl.BlockSpec((1, 2, D_MODEL), lambda b, t: (b, 0, 0)),
            pl.BlockSpec((1, D_MODEL), const2),
            pl.BlockSpec((D_MODEL, 2 * D_FF), const2, pipeline_mode=resident),
            pl.BlockSpec((FFN_CONV, D_FF), const2),
            pl.BlockSpec((1, D_FF), const2),
            pl.BlockSpec((D_FF, D_MODEL), const2, pipeline_mode=resident),
            pl.BlockSpec((1, D_MODEL), const2),
        ],
        out_specs=pl.BlockSpec((1, ts, D_MODEL), lambda b, t: (b, t, 0)),
        out_shape=jax.ShapeDtypeStruct(x.shape, F32),
        scratch_shapes=[
            pltpu.VMEM((ts, D_MODEL), BF16),
            pltpu.VMEM((HIST + ts, D_FF), F32),
            pltpu.VMEM((ts, D_FF), BF16),
        ],
        compiler_params=pltpu.CompilerParams(
            dimension_semantics=("arbitrary", "arbitrary"),
            vmem_limit_bytes=VMEM_LIMIT_BYTES),
        name="channel_mix",
    )(x, mod, fmod, n2g, w_up, fcw, fcb, w_down, fg)


def kernel(x, c, ada_w, ada_b, norm1_g, w_in, gm_vnorm_g, gm_spatial_w, gm_spatial_b, ml_conv_w,
           ml_conv_b, ml_i_b, ml_f_b, ml_hnorm_g, w_out, norm2_g, ffn_w_up, ffn_conv_w, ffn_conv_b,
           ffn_w_down, final_ada_w, final_ada_b, final_g):
    depth = ada_w.shape[0]
    assert depth == 1, "the final norm is fused into the channel-mix call of a single layer"
    bsz = x.shape[0]
    fmod = _modulation(c, final_ada_w, final_ada_b[None, :]).reshape(bsz, 2, D_MODEL)
    for l in range(depth):
        mod = _modulation(c, ada_w[l], ada_b[l][None, :]).reshape(bsz, 6, D_MODEL)

        w_main = w_in[l].astype(BF16)
        w_gate_t = jnp.zeros((16, D_MODEL), F32).at[0:8].set(w_in[l][:, N_MAIN:].T).astype(BF16)
        gate_b = jnp.broadcast_to(jnp.concatenate([ml_i_b[l], ml_f_b[l]])[:, None], (SUBLANES, 128))
        bs_col = jnp.zeros((CHUNK, 128), F32).at[:, 0:N_HEADS].set(gm_spatial_b[l].T)
        x = _token_mix(x, mod, norm1_g[l][None, :], w_main, w_gate_t, gate_b,
                       gm_vnorm_g[l][None, :], gm_spatial_w[l], bs_col, ml_conv_w[l],
                       ml_conv_b[l][None, :], ml_hnorm_g[l][None, :], w_out[l].astype(BF16), ts=512)
        x = _channel_mix(x, mod, fmod, norm2_g[l][None, :], ffn_w_up[l].astype(BF16), ffn_conv_w[l],
                         ffn_conv_b[l][None, :], ffn_w_down[l].astype(BF16), final_g[None, :],
                         ts=1024, nb=256)
    return x
```

```python
import functools
import math

import jax
import jax.numpy as jnp
from jax import lax
from jax.experimental import pallas as pl
from jax.experimental.pallas import tpu as pltpu

D_MODEL = 1024
CHUNK = 128
HEAD_DIM = 128
N_HEADS = 4
GM_WIDTH = N_HEADS * HEAD_DIM
ML_WIDTH = N_HEADS * HEAD_DIM
ML_CONV = 4
FFN_CONV = 3
D_FF = 2816
EPS = 1e-6
K_SCALE = HEAD_DIM ** -0.5
LOG_K_SCALE = -0.5 * math.log(HEAD_DIM)
N_MAIN = 2 * GM_WIDTH + 4 * ML_WIDTH
QK_BASE = 2 * GM_WIDTH
SUBLANES = 8
HIST = SUBLANES
VMEM_LIMIT_BYTES = 56 * 1024 * 1024

F32 = jnp.float32
BF16 = jnp.bfloat16


def _gelu(x):
    return 0.5 * x * (1.0 + lax.erf(x * 0.7071067811865476))


def _sigmoid(x):
    return 0.5 * jnp.tanh(0.5 * x) + 0.5


def _silu(x):
    return x * _sigmoid(x)


def _log_sigmoid(x):
    return jnp.minimum(x, 0.0) - jnp.log1p(jnp.exp(-jnp.abs(x)))


def _rms_scale(x):
    return lax.rsqrt(jnp.mean(x * x, axis=-1, keepdims=True) + EPS)


def _causal_conv(ref, row0, nrows, cols, w_ref, wcols, bias, width):
    acc = ref[row0:row0 + nrows, cols] * w_ref[width - 1:width, wcols] + bias
    for d in range(1, width):
        acc = acc + ref[row0 - d:row0 - d + nrows, cols] * w_ref[width - 1 - d:width - d, wcols]
    return acc


def _mod_kernel(c_ref, w_ref, b_ref, o_ref):
    ca = _silu(c_ref[...]).astype(BF16)
    o_ref[...] = jnp.dot(ca, w_ref[...].astype(BF16), preferred_element_type=F32) + b_ref[...]


def _modulation(c, w, b):
    bsz = c.shape[0]
    n = w.shape[1]
    bn = D_MODEL
    return pl.pallas_call(
        _mod_kernel,
        grid=(n // bn,),
        in_specs=[
            pl.BlockSpec((bsz, D_MODEL), lambda j: (0, 0)),
            pl.BlockSpec((D_MODEL, bn), lambda j: (0, j)),
            pl.BlockSpec((1, bn), lambda j: (0, j)),
        ],
        out_specs=pl.BlockSpec((bsz, bn), lambda j: (0, j)),
        out_shape=jax.ShapeDtypeStruct((bsz, n), F32),
        compiler_params=pltpu.CompilerParams(dimension_semantics=("arbitrary",)),
        name="adaln_modulation",
    )(c, w, b)


PROJ_BLOCK = 256


def _mix_kernel(xa_ref, xc_ref, moda_ref, modc_ref, n1g_ref, win_ref, wgt_ref, gb_ref, vng_ref,
                ws_ref, bs_ref, cw_ref, cb_ref, hng_ref, wout_ref, o_ref,
                hb_ref, proj0_ref, proj1_ref, gr0_ref, gr1_ref, ymix0_ref, ymix1_ref, wsb_ref,
                mst_ref, m_ref, *, ts, nt, n_tiles):
    nc = ts // CHUNK
    pair = pl.program_id(0)
    qk_cols = slice(QK_BASE, QK_BASE + 2 * ML_WIDTH)

    tri_r = lax.broadcasted_iota(jnp.int32, (CHUNK, CHUNK), 0)
    tri_c = lax.broadcasted_iota(jnp.int32, (CHUNK, CHUNK), 1)
    tril = tri_c <= tri_r

    @pl.when(pair == 0)
    def _():
        proj1_ref[...] = jnp.zeros_like(proj1_ref)
        gr1_ref[...] = jnp.zeros_like(gr1_ref)
        ymix1_ref[...] = jnp.zeros_like(ymix1_ref)
        mst_ref[...] = jnp.zeros_like(mst_ref)
        m_ref[...] = jnp.zeros_like(m_ref)
        for h in range(N_HEADS):
            wsb_ref[h] = jnp.where(tril, ws_ref[h], 0.0).astype(BF16)

    def body(proj_w, proj_r, gr_w, gr_r, ymix_w, ymix_r, half):
        s = 2 * pair + half
        t_a = lax.rem(jnp.minimum(s, n_tiles - 1), nt)
        new_seq_b = lax.rem(jnp.maximum(s - 1, 0), nt) == 0
        io_rows = half * ts

        moda = moda_ref[0]
        sh1, sc1 = moda[0:1], moda[1:2]
        gain1 = n1g_ref[...] * (1.0 + sc1)
        for c in range(nc):
            rows = slice(c * CHUNK, (c + 1) * CHUNK)
            xs = xa_ref[0, io_rows + c * CHUNK:io_rows + (c + 1) * CHUNK, :]
            hb_ref[rows, :] = (xs * _rms_scale(xs) * gain1 + sh1).astype(BF16)

        gates = lax.dot_general(wgt_ref[...], hb_ref[...], (((1,), (1,)), ((), ())),
                                preferred_element_type=F32)
        gates = gates[0:SUBLANES] + gb_ref[:, 0:1]
        logf = _log_sigmoid(pltpu.roll(gates, N_HEADS, axis=0))
        lane = lax.broadcasted_iota(jnp.int32, (SUBLANES, ts), 1) % CHUNK
        bsum = logf
        for d in (1, 2, 4, 8, 16, 32, 64):
            bsum = bsum + jnp.where(lane >= d, pltpu.roll(bsum, d, axis=1), 0.0)
        rr = gates - bsum
        crun = rr
        for d in (1, 2, 4, 8, 16, 32, 64):
            crun = jnp.maximum(crun, jnp.where(lane >= d, pltpu.roll(crun, d, axis=1), -jnp.inf))
        gr_w[0] = gates
        gr_w[1] = bsum
        gr_w[2] = rr
        gr_w[3] = crun

        hist = proj_r[ts:ts + HIST, qk_cols]
        proj_w[0:HIST, qk_cols] = jnp.where(t_a == 0, jnp.zeros_like(hist), hist)

        g1 = modc_ref[0][2:3]
        ii, b, r, cr = gr_r[0], gr_r[1], gr_r[2], gr_r[3]
        zpad = jnp.zeros((CHUNK - 3 * SUBLANES, CHUNK), F32)
        m = jnp.where(new_seq_b, 0.0, m_ref[:, 0:1])
        zts, wgts, decays = [], [], []
        for c in range(nc):
            rows = slice(c * CHUNK, (c + 1) * CHUNK)
            b_c, cr_c, ii_c = b[:, rows], cr[:, rows], ii[:, rows]
            b_last = b_c[:, CHUNK - 1:CHUNK]
            a_c = b_last - b_c + ii_c
            a_max = jnp.max(a_c, axis=1, keepdims=True)
            m_new = jnp.maximum(b_last + m, a_max)
            decays.append(jnp.exp(b_last + m - m_new))
            wgts.append(jnp.exp(a_c - a_max) * (jnp.exp(a_max - m_new) * K_SCALE))
            mm = jnp.maximum(m, cr_c)
            zts.append(jnp.concatenate([mm, jnp.exp(m - mm), jnp.exp(-(b_c + mm)), zpad], axis=0).T)
            m = m_new
        m_ref[...] = jnp.broadcast_to(m, m_ref.shape)

        ones_col = jnp.where(tri_c == 0, 1.0, 0.0).astype(BF16)

        n_in = N_MAIN // PROJ_BLOCK
        n_blocks = n_in + D_MODEL // PROJ_BLOCK
        n_items = nc * N_HEADS
        order = []
        for grp in range(D_MODEL // PROJ_BLOCK):
            order.append(n_in + grp)
            order.extend(range(grp * 3, grp * 3 + 3))
        assert sorted(order) == list(range(n_blocks))
        item = 0

        def issue_block(j):
            if j < n_in:
                pc = slice(j * PROJ_BLOCK, (j + 1) * PROJ_BLOCK)
                proj_w[HIST:HIST + ts, pc] = jnp.dot(hb_ref[...], win_ref[:, pc],
                                                     preferred_element_type=F32)
            else:
                oc = slice((j - n_in) * PROJ_BLOCK, (j - n_in + 1) * PROJ_BLOCK)
                y = jnp.dot(ymix_r[...], wout_ref[:, oc], preferred_element_type=F32)
                o_ref[0, io_rows:io_rows + ts, oc] = xc_ref[0, io_rows:io_rows + ts, oc] + g1[:, oc] * y

        for c in range(nc):
            rows = slice(c * CHUNK, (c + 1) * CHUNK)
            prow = slice(HIST + c * CHUNK, HIST + (c + 1) * CHUNK)
            zt, wgt, decay = zts[c], wgts[c], decays[c]
            r_c = r[:, rows] + LOG_K_SCALE

            for h in range(N_HEADS):
                blk_lo = (item * n_blocks + n_items - 1) // n_items
                blk_hi = ((item + 1) * n_blocks + n_items - 1) // n_items
                for j in range(blk_lo, blk_hi):
                    issue_block(order[j])
                item += 1

                hc = slice(h * HEAD_DIM, (h + 1) * HEAD_DIM)
                u = _gelu(proj_r[prow, hc])
                v = _gelu(proj_r[prow, GM_WIDTH + h * HEAD_DIM:GM_WIDTH + (h + 1) * HEAD_DIM])
                v = v * _rms_scale(v) * vng_ref[:, hc]
                mixed = (jnp.dot(wsb_ref[h], v.astype(BF16), preferred_element_type=F32)
                         + bs_ref[:, h:h + 1])
                ymix_w[rows, hc] = (u * mixed).astype(BF16)

                qc = slice(h * HEAD_DIM, (h + 1) * HEAD_DIM)
                kc = slice(ML_WIDTH + h * HEAD_DIM, ML_WIDTH + (h + 1) * HEAD_DIM)
                q = _silu(_causal_conv(proj_r, prow.start, CHUNK,
                                       slice(QK_BASE + qc.start, QK_BASE + qc.stop),
                                       cw_ref, qc, cb_ref[:, qc], ML_CONV))
                k = _silu(_causal_conv(proj_r, prow.start, CHUNK,
                                       slice(QK_BASE + kc.start, QK_BASE + kc.stop),
                                       cw_ref, kc, cb_ref[:, kc], ML_CONV))
                vbase = QK_BASE + 2 * ML_WIDTH + h * HEAD_DIM
                obase = QK_BASE + 3 * ML_WIDTH + h * HEAD_DIM
                vv = proj_r[prow, vbase:vbase + HEAD_DIM]
                og = proj_r[prow, obase:obase + HEAD_DIM]

                kt = k.T
                sc = jnp.dot(q.astype(BF16), kt.astype(BF16), preferred_element_type=F32)
                mm = zt[:, h:h + 1]
                iw = zt[:, SUBLANES + h:SUBLANES + h + 1]
                en = zt[:, 2 * SUBLANES + h:2 * SUBLANES + h + 1]
                p = jnp.exp(jnp.where(tril, r_c[h:h + 1, :] - mm, -jnp.inf))
                lhs = jnp.concatenate([(sc * p).astype(BF16), (iw * q).astype(BF16)], axis=1)
                v_aug = jnp.concatenate([vv.astype(BF16), ones_col], axis=1)
                st_prev = mst_ref[h]
                if c == 0:
                    st_prev = jnp.where(new_seq_b, jnp.zeros_like(st_prev), st_prev)
                rhs = jnp.concatenate([v_aug, st_prev.astype(BF16)], axis=0)
                nd = jnp.dot(lhs, rhs, preferred_element_type=F32)
                den = nd[:, HEAD_DIM:HEAD_DIM + 1]
                hh = nd[:, 0:HEAD_DIM] / jnp.maximum(jnp.abs(den), en)
                hn = hh * _rms_scale(hh) * hng_ref[:, hc]
                ymix_w[rows, GM_WIDTH + h * HEAD_DIM:GM_WIDTH + (h + 1) * HEAD_DIM] = (
                    _sigmoid(og) * hn).astype(BF16)

                kw = (kt * wgt[h:h + 1, :]).astype(BF16)
                st_c = jnp.dot(kw, v_aug, preferred_element_type=F32)
                mst_ref[h] = decay[h:h + 1, :] * st_prev + st_c

    @pl.when(pair >= 0)
    def _():
        body(proj0_ref, proj1_ref, gr0_ref, gr1_ref, ymix0_ref, ymix1_ref, 0)

    @pl.when(pair > -1)
    def _():
        body(proj1_ref, proj0_ref, gr1_ref, gr0_ref, ymix1_ref, ymix0_ref, 1)


def _token_mix(x, mod, n1g, w_main, w_gate_t, gate_b, vng, ws, bs_col, cw, cb, hng, w_out, ts):
    bsz, seq, _ = x.shape
    nt = seq // ts
    n_tiles = bsz * nt
    const2 = lambda s: (0, 0)

    assert nt % 2 == 0
    n_pairs, pairs_per_seq = n_tiles // 2, nt // 2

    def tile_a(k):
        j = jnp.minimum(k, n_pairs - 1)
        return (j // pairs_per_seq, j % pairs_per_seq, 0)

    def tile_c(k):
        j = jnp.maximum(k - 1, 0)
        return (j // pairs_per_seq, j % pairs_per_seq, 0)

    return pl.pallas_call(
        functools.partial(_mix_kernel, ts=ts, nt=nt, n_tiles=n_tiles),
        grid=(n_pairs + 1,),
        in_specs=[
            pl.BlockSpec((1, 2 * ts, D_MODEL), tile_a),
            pl.BlockSpec((1, 2 * ts, D_MODEL), tile_c),
            pl.BlockSpec((1, 6, D_MODEL), lambda s: (tile_a(s)[0], 0, 0)),
            pl.BlockSpec((1, 6, D_MODEL), lambda s: (tile_c(s)[0], 0, 0)),
            pl.BlockSpec((1, D_MODEL), const2),
            pl.BlockSpec((D_MODEL, N_MAIN), const2, pipeline_mode=pl.Buffered(1)),
            pl.BlockSpec((16, D_MODEL), const2),
            pl.BlockSpec((SUBLANES, 128), const2),
            pl.BlockSpec((1, GM_WIDTH), const2),
            pl.BlockSpec((N_HEADS, CHUNK, CHUNK), lambda s: (0, 0, 0)),
            pl.BlockSpec((CHUNK, 128), const2),
            pl.BlockSpec((ML_CONV, 2 * ML_WIDTH), const2),
            pl.BlockSpec((1, 2 * ML_WIDTH), const2),
            pl.BlockSpec((1, ML_WIDTH), const2),
            pl.BlockSpec((D_MODEL, D_MODEL), const2, pipeline_mode=pl.Buffered(1)),
        ],
        out_specs=pl.BlockSpec((1, 2 * ts, D_MODEL), tile_c),
        out_shape=jax.ShapeDtypeStruct(x.shape, F32),
        scratch_shapes=[
            pltpu.VMEM((ts, D_MODEL), BF16),
            pltpu.VMEM((HIST + ts, N_MAIN), F32),
            pltpu.VMEM((HIST + ts, N_MAIN), F32),
            pltpu.VMEM((4, SUBLANES, ts), F32),
            pltpu.VMEM((4, SUBLANES, ts), F32),
            pltpu.VMEM((ts, D_MODEL), BF16),
            pltpu.VMEM((ts, D_MODEL), BF16),
            pltpu.VMEM((N_HEADS, CHUNK, CHUNK), BF16),
            pltpu.VMEM((N_HEADS, HEAD_DIM, 2 * HEAD_DIM), F32),
            pltpu.VMEM((SUBLANES, 128), F32),
        ],
        compiler_params=pltpu.CompilerParams(
            dimension_semantics=("arbitrary",),
            vmem_limit_bytes=VMEM_LIMIT_BYTES),
        name="token_mix",
    )(x, x, mod, mod, n1g, w_main, w_gate_t, gate_b, vng, ws, bs_col, cw, cb, hng, w_out)


def _ffn_kernel(x_ref, mod_ref, fmod_ref, n2g_ref, wup_ref, fcw_ref, fcb_ref, wdn_ref, fg_ref, o_ref,
                hb_ref, g_ref, a_ref, *, ts, nb):
    @pl.when(pl.program_id(1) == 0)
    def _():
        g_ref[0:HIST, :] = jnp.zeros((HIST, D_FF), F32)

    mod = mod_ref[0]
    sh2, sc2, g2 = mod[3:4], mod[4:5], mod[5:6]
    fmod = fmod_ref[0]
    fsh, fsc = fmod[0:1], fmod[1:2]
    gain2 = n2g_ref[...] * (1.0 + sc2)
    gainf = fg_ref[...] * (1.0 + fsc)
    th = ts // 2

    def norm(r0):
        for c in range(th // CHUNK):
            rows = slice(r0 + c * CHUNK, r0 + (c + 1) * CHUNK)
            xs = x_ref[0, rows, :]
            hb_ref[rows, :] = (xs * _rms_scale(xs) * gain2 + sh2).astype(BF16)

    def up(r0):
        rows = slice(r0, r0 + th)
        for j in range(D_FF // nb):
            cols = slice(j * nb, (j + 1) * nb)
            g_ref[HIST + r0:HIST + r0 + th, cols] = jnp.dot(hb_ref[rows, :], wup_ref[:, cols],
                                                            preferred_element_type=F32)
            u = jnp.dot(hb_ref[rows, :], wup_ref[:, D_FF + j * nb:D_FF + (j + 1) * nb],
                        preferred_element_type=F32)
            gc = _causal_conv(g_ref, HIST + r0, th, cols, fcw_ref, cols, fcb_ref[:, cols], FFN_CONV)
            a_ref[rows, cols] = (_gelu(gc) * u).astype(BF16)

    def down(r0):
        f = jnp.dot(a_ref[r0:r0 + th, :], wdn_ref[...], preferred_element_type=F32)
        for c in range(th // CHUNK):
            rows = slice(r0 + c * CHUNK, r0 + (c + 1) * CHUNK)
            x2 = x_ref[0, rows, :] + g2 * f[c * CHUNK:(c + 1) * CHUNK, :]
            o_ref[0, rows, :] = x2 * _rms_scale(x2) * gainf + fsh

    norm(0)
    norm(th)
    up(0)
    up(th)
    g_ref[0:HIST, :] = g_ref[ts:ts + HIST, :]
    down(0)
    down(th)


def _channel_mix(x, mod, fmod, n2g, w_up, fcw, fcb, w_down, fg, ts, nb):
    bsz, seq, _ = x.shape
    const2 = lambda b, t: (0, 0)
    resident = pl.Buffered(1)
    return pl.pallas_call(
        functools.partial(_ffn_kernel, ts=ts, nb=nb),
        grid=(bsz, seq // ts),
        in_specs=[
            pl.BlockSpec((1, ts, D_MODEL), lambda b, t: (b, t, 0)),
            pl.BlockSpec((1, 6, D_MODEL), lambda b, t: (b, 0, 0)),
            pl.BlockSpec((1, 2, D_MODEL), lambda b, t: (b, 0, 0)),
            pl.BlockSpec((1, D_MODEL), const2),
            pl.BlockSpec((D_MODEL, 2 * D_FF), const2, pipeline_mode=resident),
            pl.BlockSpec((FFN_CONV, D_FF), const2),
            pl.BlockSpec((1, D_FF), const2),
            pl.BlockSpec((D_FF, D_MODEL), const2, pipeline_mode=resident),
            pl.BlockSpec((1, D_MODEL), const2),
        ],
        out_specs=pl.BlockSpec((1, ts, D_MODEL), lambda b, t: (b, t, 0)),
        out_shape=jax.ShapeDtypeStruct(x.shape, F32),
        scratch_shapes=[
            pltpu.VMEM((ts, D_MODEL), BF16),
            pltpu.VMEM((HIST + ts, D_FF), F32),
            pltpu.VMEM((ts, D_FF), BF16),
        ],
        compiler_params=pltpu.CompilerParams(
            dimension_semantics=("arbitrary", "arbitrary"),
            vmem_limit_bytes=VMEM_LIMIT_BYTES),
        name="channel_mix",
    )(x, mod, fmod, n2g, w_up, fcw, fcb, w_down, fg)


def kernel(x, c, ada_w, ada_b, norm1_g, w_in, gm_vnorm_g, gm_spatial_w, gm_spatial_b, ml_conv_w,
           ml_conv_b, ml_i_b, ml_f_b, ml_hnorm_g, w_out, norm2_g, ffn_w_up, ffn_conv_w, ffn_conv_b,
           ffn_w_down, final_ada_w, final_ada_b, final_g):
    depth = ada_w.shape[0]
    assert depth == 1, "the final norm is fused into the channel-mix call of a single layer"
    bsz = x.shape[0]
    fmod = _modulation(c, final_ada_w, final_ada_b[None, :]).reshape(bsz, 2, D_MODEL)
    for l in range(depth):
        mod = _modulation(c, ada_w[l], ada_b[l][None, :]).reshape(bsz, 6, D_MODEL)

        w_main = w_in[l].astype(BF16)
        w_gate_t = jnp.zeros((16, D_MODEL), F32).at[0:8].set(w_in[l][:, N_MAIN:].T).astype(BF16)
        gate_b = jnp.broadcast_to(jnp.concatenate([ml_i_b[l], ml_f_b[l]])[:, None], (SUBLANES, 128))
        bs_col = jnp.zeros((CHUNK, 128), F32).at[:, 0:N_HEADS].set(gm_spatial_b[l].T)
        x = _token_mix(x, mod, norm1_g[l][None, :], w_main, w_gate_t, gate_b,
                       gm_vnorm_g[l][None, :], gm_spatial_w[l], bs_col, ml_conv_w[l],
                       ml_conv_b[l][None, :], ml_hnorm_g[l][None, :], w_out[l].astype(BF16), ts=512)
        x = _channel_mix(x, mod, fmod, norm2_g[l][None, :], ffn_w_up[l].astype(BF16), ffn_conv_w[l],
                         ffn_conv_b[l][None, :], ffn_w_down[l].astype(BF16), final_g[None, :],
                         ts=1024, nb=256)
    return x
```

```python
import functools
import math

import jax
import jax.numpy as jnp
from jax import lax
from jax.experimental import pallas as pl
from jax.experimental.pallas import tpu as pltpu

D_MODEL = 1024
CHUNK = 128
HEAD_DIM = 128
N_HEADS = 4
GM_WIDTH = N_HEADS * HEAD_DIM
ML_WIDTH = N_HEADS * HEAD_DIM
ML_CONV = 4
FFN_CONV = 3
D_FF = 2816
EPS = 1e-6
K_SCALE = HEAD_DIM ** -0.5
LOG_K_SCALE = -0.5 * math.log(HEAD_DIM)
N_MAIN = 2 * GM_WIDTH + 4 * ML_WIDTH
QK_BASE = 2 * GM_WIDTH
SUBLANES = 8
BF16_SUBLANES = 16
LANES = 128
HIST = SUBLANES
VMEM_LIMIT_BYTES = 56 * 1024 * 1024
TOKEN_TILE = 512
FFN_TILE = 1024
FFN_BLOCK = 256

F32 = jnp.float32
BF16 = jnp.bfloat16


def _gelu(x):
    return 0.5 * x * (1.0 + lax.erf(x * 0.7071067811865476))


def _sigmoid(x):
    return 0.5 * jnp.tanh(0.5 * x) + 0.5


def _silu(x):
    return x * _sigmoid(x)


def _log_sigmoid(x):
    return jnp.minimum(x, 0.0) - jnp.log1p(jnp.exp(-jnp.abs(x)))


def _rms_scale(x):
    return lax.rsqrt(jnp.mean(x * x, axis=-1, keepdims=True) + EPS)


def _causal_conv(ref, row0, nrows, cols, w_ref, wcols, bias, width):
    acc = ref[row0:row0 + nrows, cols] * w_ref[width - 1:width, wcols] + bias
    for d in range(1, width):
        acc = acc + ref[row0 - d:row0 - d + nrows, cols] * w_ref[width - 1 - d:width - d, wcols]
    return acc


def _mod_kernel(c_ref, w_ref, b_ref, o_ref):
    ca = _silu(c_ref[...]).astype(BF16)
    o_ref[...] = jnp.dot(ca, w_ref[...].astype(BF16), preferred_element_type=F32) + b_ref[...]


def _modulation(c, w, b):
    bsz = c.shape[0]
    n = w.shape[1]
    bn = D_MODEL
    return pl.pallas_call(
        _mod_kernel,
        grid=(n // bn,),
        in_specs=[
            pl.BlockSpec((bsz, D_MODEL), lambda j: (0, 0)),
            pl.BlockSpec((D_MODEL, bn), lambda j: (0, j)),
            pl.BlockSpec((1, bn), lambda j: (0, j)),
        ],
        out_specs=pl.BlockSpec((bsz, bn), lambda j: (0, j)),
        out_shape=jax.ShapeDtypeStruct((bsz, n), F32),
        compiler_params=pltpu.CompilerParams(dimension_semantics=("arbitrary",)),
        name="adaln_modulation",
    )(c, w, b)


PROJ_BLOCK = 256


def _mix_kernel(xa_ref, xc_ref, moda_ref, modc_ref, n1g_ref, win_ref, wgt_ref, gb_ref, vng_ref,
                ws_ref, bs_ref, cw_ref, cb_ref, hng_ref, wout_ref, o_ref,
                hb_ref, proj0_ref, proj1_ref, gr0_ref, gr1_ref, ymix0_ref, ymix1_ref, wsb_ref,
                mst_ref, m_ref, *, ts, nt, n_tiles):
    nc = ts // CHUNK
    pair = pl.program_id(0)
    qk_cols = slice(QK_BASE, QK_BASE + 2 * ML_WIDTH)

    tri_r = lax.broadcasted_iota(jnp.int32, (CHUNK, CHUNK), 0)
    tri_c = lax.broadcasted_iota(jnp.int32, (CHUNK, CHUNK), 1)
    tril = tri_c <= tri_r

    @pl.when(pair == 0)
    def _():
        proj1_ref[...] = jnp.zeros_like(proj1_ref)
        gr1_ref[...] = jnp.zeros_like(gr1_ref)
        ymix1_ref[...] = jnp.zeros_like(ymix1_ref)
        mst_ref[...] = jnp.zeros_like(mst_ref)
        m_ref[...] = jnp.zeros_like(m_ref)
        for h in range(N_HEADS):
            wsb_ref[h] = jnp.where(tril, ws_ref[h], 0.0).astype(BF16)

    def body(proj_w, proj_r, gr_w, gr_r, ymix_w, ymix_r, half):
        s = 2 * pair + half
        t_a = lax.rem(jnp.minimum(s, n_tiles - 1), nt)
        new_seq_b = lax.rem(jnp.maximum(s - 1, 0), nt) == 0
        io_rows = half * ts
        n_in = N_MAIN // PROJ_BLOCK
        n_out = D_MODEL // PROJ_BLOCK
        g1 = modc_ref[0][2:3]

        def issue_block(j):
            if j < n_in:
                pc = slice(j * PROJ_BLOCK, (j + 1) * PROJ_BLOCK)
                proj_w[HIST:HIST + ts, pc] = jnp.dot(hb_ref[...], win_ref[:, pc],
                                                     preferred_element_type=F32)
            else:
                oc = slice((j - n_in) * PROJ_BLOCK, (j - n_in + 1) * PROJ_BLOCK)
                y = jnp.dot(ymix_r[...], wout_ref[:, oc], preferred_element_type=F32)
                o_ref[0, io_rows:io_rows + ts, oc] = xc_ref[0, io_rows:io_rows + ts, oc] + g1[:, oc] * y

        early_c = 2
        for j in range(early_c):
            issue_block(n_in + j)

        moda = moda_ref[0]
        sh1, sc1 = moda[0:1], moda[1:2]
        gain1 = n1g_ref[...] * (1.0 + sc1)
        for c in range(nc):
            rows = slice(c * CHUNK, (c + 1) * CHUNK)
            xs = xa_ref[0, io_rows + c * CHUNK:io_rows + (c + 1) * CHUNK, :]
            hb_ref[rows, :] = (xs * _rms_scale(xs) * gain1 + sh1).astype(BF16)

        gates = lax.dot_general(wgt_ref[...], hb_ref[...], (((1,), (1,)), ((), ())),
                                preferred_element_type=F32)
        gates = gates[0:SUBLANES] + gb_ref[:, 0:1]
        logf = _log_sigmoid(pltpu.roll(gates, N_HEADS, axis=0))
        lane = lax.broadcasted_iota(jnp.int32, (SUBLANES, ts), 1) % CHUNK
        bsum = logf
        for d in (1, 2, 4, 8, 16, 32, 64):
            bsum = bsum + jnp.where(lane >= d, pltpu.roll(bsum, d, axis=1), 0.0)
        rr = gates - bsum
        crun = rr
        for d in (1, 2, 4, 8, 16, 32, 64):
            crun = jnp.maximum(crun, jnp.where(lane >= d, pltpu.roll(crun, d, axis=1), -jnp.inf))
        gr_w[0] = gates
        gr_w[1] = bsum
        gr_w[2] = rr
        gr_w[3] = crun

        hist = proj_r[ts:ts + HIST, qk_cols]
        proj_w[0:HIST, qk_cols] = jnp.where(t_a == 0, jnp.zeros_like(hist), hist)

        ii, b, r, cr = gr_r[0], gr_r[1], gr_r[2], gr_r[3]
        zpad = jnp.zeros((CHUNK - 3 * SUBLANES, CHUNK), F32)
        m = jnp.where(new_seq_b, 0.0, m_ref[:, 0:1])
        zts, wgts, decays = [], [], []
        for c in range(nc):
            rows = slice(c * CHUNK, (c + 1) * CHUNK)
            b_c, cr_c, ii_c = b[:, rows], cr[:, rows], ii[:, rows]
            b_last = b_c[:, CHUNK - 1:CHUNK]
            a_c = b_last - b_c + ii_c
            a_max = jnp.max(a_c, axis=1, keepdims=True)
            m_new = jnp.maximum(b_last + m, a_max)
            decays.append(jnp.exp(b_last + m - m_new))
            wgts.append(jnp.exp(a_c - a_max) * (jnp.exp(a_max - m_new) * K_SCALE))
            mm = jnp.maximum(m, cr_c)
            zts.append(jnp.concatenate([mm, jnp.exp(m - mm), jnp.exp(-(b_c + mm)), zpad], axis=0).T)
            m = m_new
        m_ref[...] = jnp.broadcast_to(m, m_ref.shape)

        ones_col = jnp.where(tri_c == 0, 1.0, 0.0).astype(BF16)

        n_items = nc * N_HEADS
        late_c = list(range(n_in + early_c, n_in + n_out))
        per = -(-n_in // (len(late_c) + 1))
        order = []
        for grp in range(len(late_c) + 1):
            order.extend(range(grp * per, min((grp + 1) * per, n_in)))
            if grp < len(late_c):
                order.append(late_c[grp])
        assert sorted(order) == list(range(n_in)) + late_c
        n_blocks = len(order)
        item = 0

        for c in range(nc):
            rows = slice(c * CHUNK, (c + 1) * CHUNK)
            prow = slice(HIST + c * CHUNK, HIST + (c + 1) * CHUNK)
            zt, wgt, decay = zts[c], wgts[c], decays[c]
            r_c = r[:, rows] + LOG_K_SCALE

            for h in range(N_HEADS):
                blk_lo = (item * n_blocks + n_items - 1) // n_items
                blk_hi = ((item + 1) * n_blocks + n_items - 1) // n_items
                for j in range(blk_lo, blk_hi):
                    issue_block(order[j])
                item += 1

                hc = slice(h * HEAD_DIM, (h + 1) * HEAD_DIM)
                u = _gelu(proj_r[prow, hc])
                v = _gelu(proj_r[prow, GM_WIDTH + h * HEAD_DIM:GM_WIDTH + (h + 1) * HEAD_DIM])
                v = v * _rms_scale(v) * vng_ref[:, hc]
                mixed = (jnp.dot(wsb_ref[h], v.astype(BF16), preferred_element_type=F32)
                         + bs_ref[:, h:h + 1])
                ymix_w[rows, hc] = (u * mixed).astype(BF16)

                qc = slice(h * HEAD_DIM, (h + 1) * HEAD_DIM)
                kc = slice(ML_WIDTH + h * HEAD_DIM, ML_WIDTH + (h + 1) * HEAD_DIM)
                q = _silu(_causal_conv(proj_r, prow.start, CHUNK,
                                       slice(QK_BASE + qc.start, QK_BASE + qc.stop),
                                       cw_ref, qc, cb_ref[:, qc], ML_CONV))
                k = _silu(_causal_conv(proj_r, prow.start, CHUNK,
                                       slice(QK_BASE + kc.start, QK_BASE + kc.stop),
                                       cw_ref, kc, cb_ref[:, kc], ML_CONV))
                vbase = QK_BASE + 2 * ML_WIDTH + h * HEAD_DIM
                obase = QK_BASE + 3 * ML_WIDTH + h * HEAD_DIM
                vv = proj_r[prow, vbase:vbase + HEAD_DIM]
                og = proj_r[prow, obase:obase + HEAD_DIM]

                kt = k.T
                sc = jnp.dot(q.astype(BF16), kt.astype(BF16), preferred_element_type=F32)
                mm = zt[:, h:h + 1]
                iw = zt[:, SUBLANES + h:SUBLANES + h + 1]
                en = zt[:, 2 * SUBLANES + h:2 * SUBLANES + h + 1]
                p = jnp.exp(jnp.where(tril, r_c[h:h + 1, :] - mm, -jnp.inf))
                lhs = jnp.concatenate([(sc * p).astype(BF16), (iw * q).astype(BF16)], axis=1)
                v_aug = jnp.concatenate([vv.astype(BF16), ones_col], axis=1)
                st_prev = mst_ref[h]
                if c == 0:
                    st_prev = jnp.where(new_seq_b, jnp.zeros_like(st_prev), st_prev)
                rhs = jnp.concatenate([v_aug, st_prev.astype(BF16)], axis=0)
                nd = jnp.dot(lhs, rhs, preferred_element_type=F32)
                den = nd[:, HEAD_DIM:HEAD_DIM + 1]
                hh = nd[:, 0:HEAD_DIM] / jnp.maximum(jnp.abs(den), en)
                hn = hh * _rms_scale(hh) * hng_ref[:, hc]
                ymix_w[rows, GM_WIDTH + h * HEAD_DIM:GM_WIDTH + (h + 1) * HEAD_DIM] = (
                    _sigmoid(og) * hn).astype(BF16)

                kw = (kt * wgt[h:h + 1, :]).astype(BF16)
                st_c = jnp.dot(kw, v_aug, preferred_element_type=F32)
                mst_ref[h] = decay[h:h + 1, :] * st_prev + st_c

    @pl.when(pair >= 0)
    def _():
        body(proj0_ref, proj1_ref, gr0_ref, gr1_ref, ymix0_ref, ymix1_ref, 0)

    @pl.when(pair > -1)
    def _():
        body(proj1_ref, proj0_ref, gr1_ref, gr0_ref, ymix1_ref, ymix0_ref, 1)


def _token_mix(x, mod, n1g, w_main, w_gate_t, gate_b, vng, ws, bs_col, cw, cb, hng, w_out, ts):
    bsz, seq, _ = x.shape
    nt = seq // ts
    n_tiles = bsz * nt
    const2 = lambda s: (0, 0)

    assert nt % 2 == 0
    n_pairs, pairs_per_seq = n_tiles // 2, nt // 2

    def tile_a(k):
        j = jnp.minimum(k, n_pairs - 1)
        return (j // pairs_per_seq, j % pairs_per_seq, 0)

    def tile_c(k):
        j = jnp.maximum(k - 1, 0)
        return (j // pairs_per_seq, j % pairs_per_seq, 0)

    return pl.pallas_call(
        functools.partial(_mix_kernel, ts=ts, nt=nt, n_tiles=n_tiles),
        grid=(n_pairs + 1,),
        in_specs=[
            pl.BlockSpec((1, 2 * ts, D_MODEL), tile_a),
            pl.BlockSpec((1, 2 * ts, D_MODEL), tile_c),
            pl.BlockSpec((1, 6, D_MODEL), lambda s: (tile_a(s)[0], 0, 0)),
            pl.BlockSpec((1, 6, D_MODEL), lambda s: (tile_c(s)[0], 0, 0)),
            pl.BlockSpec((1, D_MODEL), const2),
            pl.BlockSpec((D_MODEL, N_MAIN), const2, pipeline_mode=pl.Buffered(1)),
            pl.BlockSpec((BF16_SUBLANES, D_MODEL), const2),
            pl.BlockSpec((SUBLANES, LANES), const2),
            pl.BlockSpec((1, GM_WIDTH), const2),
            pl.BlockSpec((N_HEADS, CHUNK, CHUNK), lambda s: (0, 0, 0)),
            pl.BlockSpec((CHUNK, LANES), const2),
            pl.BlockSpec((ML_CONV, 2 * ML_WIDTH), const2),
            pl.BlockSpec((1, 2 * ML_WIDTH), const2),
            pl.BlockSpec((1, ML_WIDTH), const2),
            pl.BlockSpec((D_MODEL, D_MODEL), const2, pipeline_mode=pl.Buffered(1)),
        ],
        out_specs=pl.BlockSpec((1, 2 * ts, D_MODEL), tile_c),
        out_shape=jax.ShapeDtypeStruct(x.shape, F32),
        scratch_shapes=[
            pltpu.VMEM((ts, D_MODEL), BF16),
            pltpu.VMEM((HIST + ts, N_MAIN), F32),
            pltpu.VMEM((HIST + ts, N_MAIN), F32),
            pltpu.VMEM((4, SUBLANES, ts), F32),
            pltpu.VMEM((4, SUBLANES, ts), F32),
            pltpu.VMEM((ts, D_MODEL), BF16),
            pltpu.VMEM((ts, D_MODEL), BF16),
            pltpu.VMEM((N_HEADS, CHUNK, CHUNK), BF16),
            pltpu.VMEM((N_HEADS, HEAD_DIM, 2 * HEAD_DIM), F32),
            pltpu.VMEM((SUBLANES, LANES), F32),
        ],
        compiler_params=pltpu.CompilerParams(
            dimension_semantics=("arbitrary",),
            vmem_limit_bytes=VMEM_LIMIT_BYTES),
        name="token_mix",
    )(x, x, mod, mod, n1g, w_main, w_gate_t, gate_b, vng, ws, bs_col, cw, cb, hng, w_out)


def _ffn_kernel(x_ref, mod_ref, fmod_ref, n2g_ref, wup_ref, fcw_ref, fcb_ref, wdn_ref, fg_ref, o_ref,
                hb_ref, g_ref, a_ref, *, ts, nb):
    @pl.when(pl.program_id(1) == 0)
    def _():
        g_ref[0:HIST, :] = jnp.zeros((HIST, D_FF), F32)

    mod = mod_ref[0]
    sh2, sc2, g2 = mod[3:4], mod[4:5], mod[5:6]
    fmod = fmod_ref[0]
    fsh, fsc = fmod[0:1], fmod[1:2]
    gain2 = n2g_ref[...] * (1.0 + sc2)
    gainf = fg_ref[...] * (1.0 + fsc)
    th = ts // 2

    def norm(r0):
        for c in range(th // CHUNK):
            rows = slice(r0 + c * CHUNK, r0 + (c + 1) * CHUNK)
            xs = x_ref[0, rows, :]
            hb_ref[rows, :] = (xs * _rms_scale(xs) * gain2 + sh2).astype(BF16)

    def up(r0):
        rows = slice(r0, r0 + th)
        for j in range(D_FF // nb):
            cols = slice(j * nb, (j + 1) * nb)
            g_ref[HIST + r0:HIST + r0 + th, cols] = jnp.dot(hb_ref[rows, :], wup_ref[:, cols],
                                                            preferred_element_type=F32)
            u = jnp.dot(hb_ref[rows, :], wup_ref[:, D_FF + j * nb:D_FF + (j + 1) * nb],
                        preferred_element_type=F32)
            gc = _causal_conv(g_ref, HIST + r0, th, cols, fcw_ref, cols, fcb_ref[:, cols], FFN_CONV)
            a_ref[rows, cols] = (_gelu(gc) * u).astype(BF16)

    def down(r0):
        f = jnp.dot(a_ref[r0:r0 + th, :], wdn_ref[...], preferred_element_type=F32)
        for c in range(th // CHUNK):
            rows = slice(r0 + c * CHUNK, r0 + (c + 1) * CHUNK)
            x2 = x_ref[0, rows, :] + g2 * f[c * CHUNK:(c + 1) * CHUNK, :]
            o_ref[0, rows, :] = x2 * _rms_scale(x2) * gainf + fsh

    norm(0)
    norm(th)
    up(0)
    up(th)
    g_ref[0:HIST, :] = g_ref[ts:ts + HIST, :]
    down(0)
    down(th)


def _channel_mix(x, mod, fmod, n2g, w_up, fcw, fcb, w_down, fg, ts, nb):
    bsz, seq, _ = x.shape
    const2 = lambda b, t: (0, 0)
    resident = pl.Buffered(1)
    return pl.pallas_call(
        functools.partial(_ffn_kernel, ts=ts, nb=nb),
        grid=(bsz, seq // ts),
        in_specs=[
            pl.BlockSpec((1, ts, D_MODEL), lambda b, t: (b, t, 0)),
            pl.BlockSpec((1, 6, D_MODEL), lambda b, t: (b, 0, 0)),
            pl.BlockSpec((1, 2, D_MODEL), lambda b, t: (b, 0, 0)),
            pl.BlockSpec((1, D_MODEL), const2),
            pl.BlockSpec((D_MODEL, 2 * D_FF), const2, pipeline_mode=resident),
            pl.BlockSpec((FFN_CONV, D_FF), const2),
            pl.BlockSpec((1, D_FF), const2),
            pl.BlockSpec((D_FF, D_MODEL), const2, pipeline_mode=resident),
            pl.BlockSpec((1, D_MODEL), const2),
        ],
        out_specs=pl.BlockSpec((1, ts, D_MODEL), lambda b, t: (b, t, 0)),
        out_shape=jax.ShapeDtypeStruct(x.shape, F32),
        scratch_shapes=[
            pltpu.VMEM((ts, D_MODEL), BF16),
            pltpu.VMEM((HIST + ts, D_FF), F32),
            pltpu.VMEM((ts, D_FF), BF16),
        ],
        compiler_params=pltpu.CompilerParams(
            dimension_semantics=("arbitrary", "arbitrary"),
            vmem_limit_bytes=VMEM_LIMIT_BYTES),
        name="channel_mix",
    )(x, mod, fmod, n2g, w_up, fcw, fcb, w_down, fg)


def kernel(x, c, ada_w, ada_b, norm1_g, w_in, gm_vnorm_g, gm_spatial_w, gm_spatial_b, ml_conv_w,
           ml_conv_b, ml_i_b, ml_f_b, ml_hnorm_g, w_out, norm2_g, ffn_w_up, ffn_conv_w, ffn_conv_b,
           ffn_w_down, final_ada_w, final_ada_b, final_g):
    depth = ada_w.shape[0]
    assert depth == 1, "the final norm is fused into the channel-mix call of a single layer"
    bsz = x.shape[0]
    fmod = _modulation(c, final_ada_w, final_ada_b[None, :]).reshape(bsz, 2, D_MODEL)
    for l in range(depth):
        mod = _modulation(c, ada_w[l], ada_b[l][None, :]).reshape(bsz, 6, D_MODEL)

        w_main = w_in[l].astype(BF16)
        w_gate_t = (jnp.zeros((BF16_SUBLANES, D_MODEL), F32).at[0:2 * N_HEADS]
                    .set(w_in[l][:, N_MAIN:].T).astype(BF16))
        gate_b = jnp.broadcast_to(jnp.concatenate([ml_i_b[l], ml_f_b[l]])[:, None], (SUBLANES, LANES))
        bs_col = jnp.zeros((CHUNK, LANES), F32).at[:, 0:N_HEADS].set(gm_spatial_b[l].T)
        x = _token_mix(x, mod, norm1_g[l][None, :], w_main, w_gate_t, gate_b,
                       gm_vnorm_g[l][None, :], gm_spatial_w[l], bs_col, ml_conv_w[l],
                       ml_conv_b[l][None, :], ml_hnorm_g[l][None, :], w_out[l].astype(BF16), ts=TOKEN_TILE)
        x = _channel_mix(x, mod, fmod, norm2_g[l][None, :], ffn_w_up[l].astype(BF16), ffn_conv_w[l],
                         ffn_conv_b[l][None, :], ffn_w_down[l].astype(BF16), final_g[None, :],
                         ts=FFN_TILE, nb=FFN_BLOCK)
    return x
```

```python
import functools
import math

import jax
import jax.numpy as jnp
from jax import lax
from jax.experimental import pallas as pl
from jax.experimental.pallas import tpu as pltpu

D_MODEL = 1024
CHUNK = 128
HEAD_DIM = 128
N_HEADS = 4
GM_WIDTH = N_HEADS * HEAD_DIM
ML_WIDTH = N_HEADS * HEAD_DIM
ML_CONV = 4
FFN_CONV = 3
D_FF = 2816
EPS = 1e-6
K_SCALE = HEAD_DIM ** -0.5
LOG_K_SCALE = -0.5 * math.log(HEAD_DIM)
N_MAIN = 2 * GM_WIDTH + 4 * ML_WIDTH
QK_BASE = 2 * GM_WIDTH
SUBLANES = 8
BF16_SUBLANES = 16
LANES = 128
HIST = SUBLANES
VMEM_LIMIT_BYTES = 56 * 1024 * 1024
TOKEN_TILE = 512
FFN_TILE = 1024
FFN_BLOCK = 256

F32 = jnp.float32
BF16 = jnp.bfloat16


def _gelu(x):
    return 0.5 * x * (1.0 + lax.erf(x * 0.7071067811865476))


def _sigmoid(x):
    return 0.5 * jnp.tanh(0.5 * x) + 0.5


def _silu(x):
    return x * _sigmoid(x)


def _log_sigmoid(x):
    return jnp.minimum(x, 0.0) - jnp.log1p(jnp.exp(-jnp.abs(x)))


def _rms_scale(x):
    return lax.rsqrt(jnp.mean(x * x, axis=-1, keepdims=True) + EPS)


def _causal_conv(ref, row0, nrows, cols, w_ref, wcols, bias, width):
    acc = ref[row0:row0 + nrows, cols] * w_ref[width - 1:width, wcols] + bias
    for d in range(1, width):
        acc = acc + ref[row0 - d:row0 - d + nrows, cols] * w_ref[width - 1 - d:width - d, wcols]
    return acc


def _mod_kernel(c_ref, w_ref, b_ref, o_ref):
    ca = _silu(c_ref[...]).astype(BF16)
    o_ref[...] = jnp.dot(ca, w_ref[...].astype(BF16), preferred_element_type=F32) + b_ref[...]


def _modulation(c, w, b):
    bsz = c.shape[0]
    n = w.shape[1]
    bn = D_MODEL
    return pl.pallas_call(
        _mod_kernel,
        grid=(n // bn,),
        in_specs=[
            pl.BlockSpec((bsz, D_MODEL), lambda j: (0, 0)),
            pl.BlockSpec((D_MODEL, bn), lambda j: (0, j)),
            pl.BlockSpec((1, bn), lambda j: (0, j)),
        ],
        out_specs=pl.BlockSpec((bsz, bn), lambda j: (0, j)),
        out_shape=jax.ShapeDtypeStruct((bsz, n), F32),
        compiler_params=pltpu.CompilerParams(dimension_semantics=("arbitrary",)),
        name="adaln_modulation",
    )(c, w, b)


PROJ_BLOCK = 256


def _mix_kernel(xa_ref, xc_ref, moda_ref, modc_ref, n1g_ref, win_ref, wgt_ref, gb_ref, vng_ref,
                ws_ref, bs_ref, cw_ref, cb_ref, hng_ref, wout_ref, o_ref,
                hb_ref, proj0_ref, proj1_ref, gr0_ref, gr1_ref, ymix0_ref, ymix1_ref, wsb_ref,
                mst_ref, m_ref, *, ts, nt, n_tiles):
    nc = ts // CHUNK
    pair = pl.program_id(0)
    qk_cols = slice(QK_BASE, QK_BASE + 2 * ML_WIDTH)

    tri_r = lax.broadcasted_iota(jnp.int32, (CHUNK, CHUNK), 0)
    tri_c = lax.broadcasted_iota(jnp.int32, (CHUNK, CHUNK), 1)
    tril = tri_c <= tri_r

    @pl.when(pair == 0)
    def _():
        proj1_ref[...] = jnp.zeros_like(proj1_ref)
        gr1_ref[...] = jnp.zeros_like(gr1_ref)
        ymix1_ref[...] = jnp.zeros_like(ymix1_ref)
        mst_ref[...] = jnp.zeros_like(mst_ref)
        m_ref[...] = jnp.zeros_like(m_ref)
        for h in range(N_HEADS):
            wsb_ref[h] = jnp.where(tril, ws_ref[h], 0.0).astype(BF16)

    def body(proj_w, proj_r, gr_w, gr_r, ymix_w, ymix_r, half):
        s = 2 * pair + half
        t_a = lax.rem(jnp.minimum(s, n_tiles - 1), nt)
        new_seq_b = lax.rem(jnp.maximum(s - 1, 0), nt) == 0
        io_rows = half * ts
        n_in = N_MAIN // PROJ_BLOCK
        n_out = D_MODEL // PROJ_BLOCK
        g1 = modc_ref[0][2:3]

        def issue_block(j):
            if j < n_in:
                pc = slice(j * PROJ_BLOCK, (j + 1) * PROJ_BLOCK)
                proj_w[HIST:HIST + ts, pc] = jnp.dot(hb_ref[...], win_ref[:, pc],
                                                     preferred_element_type=F32)
            else:
                oc = slice((j - n_in) * PROJ_BLOCK, (j - n_in + 1) * PROJ_BLOCK)
                y = jnp.dot(ymix_r[...], wout_ref[:, oc], preferred_element_type=F32)
                o_ref[0, io_rows:io_rows + ts, oc] = xc_ref[0, io_rows:io_rows + ts, oc] + g1[:, oc] * y

        early_c = 2
        for j in range(early_c):
            issue_block(n_in + j)

        moda = moda_ref[0]
        sh1, sc1 = moda[0:1], moda[1:2]
        gain1 = n1g_ref[...] * (1.0 + sc1)
        for c in range(nc):
            rows = slice(c * CHUNK, (c + 1) * CHUNK)
            xs = xa_ref[0, io_rows + c * CHUNK:io_rows + (c + 1) * CHUNK, :]
            hb_ref[rows, :] = (xs * _rms_scale(xs) * gain1 + sh1).astype(BF16)

        gates = lax.dot_general(wgt_ref[...], hb_ref[...], (((1,), (1,)), ((), ())),
                                preferred_element_type=F32)
        gates = gates[0:SUBLANES] + gb_ref[:, 0:1]
        logf = _log_sigmoid(pltpu.roll(gates, N_HEADS, axis=0))
        lane = lax.broadcasted_iota(jnp.int32, (SUBLANES, ts), 1) % CHUNK
        bsum = logf
        for d in (1, 2, 4, 8, 16, 32, 64):
            bsum = bsum + jnp.where(lane >= d, pltpu.roll(bsum, d, axis=1), 0.0)
        rr = gates - bsum
        crun = rr
        for d in (1, 2, 4, 8, 16, 32, 64):
            crun = jnp.maximum(crun, jnp.where(lane >= d, pltpu.roll(crun, d, axis=1), -jnp.inf))
        gr_w[0] = gates
        gr_w[1] = bsum
        gr_w[2] = rr
        gr_w[3] = crun

        hist = proj_r[ts:ts + HIST, qk_cols]
        proj_w[0:HIST, qk_cols] = jnp.where(t_a == 0, jnp.zeros_like(hist), hist)

        ii, b, r, cr = gr_r[0], gr_r[1], gr_r[2], gr_r[3]
        zpad = jnp.zeros((CHUNK - 3 * SUBLANES, CHUNK), F32)
        m = jnp.where(new_seq_b, 0.0, m_ref[:, 0:1])
        zts, wgts, decays = [], [], []
        for c in range(nc):
            rows = slice(c * CHUNK, (c + 1) * CHUNK)
            b_c, cr_c, ii_c = b[:, rows], cr[:, rows], ii[:, rows]
            b_last = b_c[:, CHUNK - 1:CHUNK]
            a_c = b_last - b_c + ii_c
            a_max = jnp.max(a_c, axis=1, keepdims=True)
            m_new = jnp.maximum(b_last + m, a_max)
            decays.append(jnp.exp(b_last + m - m_new))
            wgts.append(jnp.exp(a_c - a_max) * (jnp.exp(a_max - m_new) * K_SCALE))
            mm = jnp.maximum(m, cr_c)
            zts.append(jnp.concatenate([mm, jnp.exp(m - mm), jnp.exp(-(b_c + mm)), zpad], axis=0).T)
            m = m_new
        m_ref[...] = jnp.broadcast_to(m, m_ref.shape)

        ones_col = jnp.where(tri_c == 0, 1.0, 0.0).astype(BF16)

        n_items = nc * N_HEADS
        late_c = list(range(n_in + early_c, n_in + n_out))
        per = -(-n_in // (len(late_c) + 1))
        order = []
        for grp in range(len(late_c) + 1):
            order.extend(range(grp * per, min((grp + 1) * per, n_in)))
            if grp < len(late_c):
                order.append(late_c[grp])
        assert sorted(order) == list(range(n_in)) + late_c
        n_blocks = len(order)

        def item_front(c, h):
            rows = slice(c * CHUNK, (c + 1) * CHUNK)
            prow = slice(HIST + c * CHUNK, HIST + (c + 1) * CHUNK)
            hc = slice(h * HEAD_DIM, (h + 1) * HEAD_DIM)
            u = _gelu(proj_r[prow, hc])
            v = _gelu(proj_r[prow, GM_WIDTH + h * HEAD_DIM:GM_WIDTH + (h + 1) * HEAD_DIM])
            v = v * _rms_scale(v) * vng_ref[:, hc]
            mixed = (jnp.dot(wsb_ref[h], v.astype(BF16), preferred_element_type=F32)
                     + bs_ref[:, h:h + 1])
            ymix_w[rows, hc] = (u * mixed).astype(BF16)
            qc = slice(h * HEAD_DIM, (h + 1) * HEAD_DIM)
            kc = slice(ML_WIDTH + h * HEAD_DIM, ML_WIDTH + (h + 1) * HEAD_DIM)
            q = _silu(_causal_conv(proj_r, prow.start, CHUNK,
                                   slice(QK_BASE + qc.start, QK_BASE + qc.stop),
                                   cw_ref, qc, cb_ref[:, qc], ML_CONV))
            k = _silu(_causal_conv(proj_r, prow.start, CHUNK,
                                   slice(QK_BASE + kc.start, QK_BASE + kc.stop),
                                   cw_ref, kc, cb_ref[:, kc], ML_CONV))
            kt = k.T
            sc = jnp.dot(q.astype(BF16), kt.astype(BF16), preferred_element_type=F32)
            return q, kt, sc

        def item_back(c, h, q, kt, sc):
            rows = slice(c * CHUNK, (c + 1) * CHUNK)
            prow = slice(HIST + c * CHUNK, HIST + (c + 1) * CHUNK)
            hc = slice(h * HEAD_DIM, (h + 1) * HEAD_DIM)
            zt, wgt, decay = zts[c], wgts[c], decays[c]
            r_c = r[:, rows] + LOG_K_SCALE
            vbase = QK_BASE + 2 * ML_WIDTH + h * HEAD_DIM
            obase = QK_BASE + 3 * ML_WIDTH + h * HEAD_DIM
            vv = proj_r[prow, vbase:vbase + HEAD_DIM]
            og = proj_r[prow, obase:obase + HEAD_DIM]
            mm = zt[:, h:h + 1]
            iw = zt[:, SUBLANES + h:SUBLANES + h + 1]
            en = zt[:, 2 * SUBLANES + h:2 * SUBLANES + h + 1]
            p = jnp.exp(jnp.where(tril, r_c[h:h + 1, :] - mm, -jnp.inf))
            lhs = jnp.concatenate([(sc * p).astype(BF16), (iw * q).astype(BF16)], axis=1)
            v_aug = jnp.concatenate([vv.astype(BF16), ones_col], axis=1)
            st_prev = mst_ref[h]
            if c == 0:
                st_prev = jnp.where(new_seq_b, jnp.zeros_like(st_prev), st_prev)
            rhs = jnp.concatenate([v_aug, st_prev.astype(BF16)], axis=0)
            nd = jnp.dot(lhs, rhs, preferred_element_type=F32)
            den = nd[:, HEAD_DIM:HEAD_DIM + 1]
            hh = nd[:, 0:HEAD_DIM] / jnp.maximum(jnp.abs(den), en)
            hn = hh * _rms_scale(hh) * hng_ref[:, hc]
            ymix_w[rows, GM_WIDTH + h * HEAD_DIM:GM_WIDTH + (h + 1) * HEAD_DIM] = (
                _sigmoid(og) * hn).astype(BF16)
            kw = (kt * wgt[h:h + 1, :]).astype(BF16)
            st_c = jnp.dot(kw, v_aug, preferred_element_type=F32)
            mst_ref[h] = decay[h:h + 1, :] * st_prev + st_c

        items = [(c, h) for c in range(nc) for h in range(N_HEADS)]
        front = item_front(*items[0])
        for item, (c, h) in enumerate(items):
            blk_lo = (item * n_blocks + n_items - 1) // n_items
            blk_hi = ((item + 1) * n_blocks + n_items - 1) // n_items
            for j in range(blk_lo, blk_hi):
                issue_block(order[j])
            nxt = item_front(*items[item + 1]) if item + 1 < n_items else None
            item_back(c, h, *front)
            front = nxt

    @pl.when(pair >= 0)
    def _():
        body(proj0_ref, proj1_ref, gr0_ref, gr1_ref, ymix0_ref, ymix1_ref, 0)

    @pl.when(pair > -1)
    def _():
        body(proj1_ref, proj0_ref, gr1_ref, gr0_ref, ymix1_ref, ymix0_ref, 1)


def _token_mix(x, mod, n1g, w_main, w_gate_t, gate_b, vng, ws, bs_col, cw, cb, hng, w_out, ts):
    bsz, seq, _ = x.shape
    nt = seq // ts
    n_tiles = bsz * nt
    const2 = lambda s: (0, 0)

    assert nt % 2 == 0
    n_pairs, pairs_per_seq = n_tiles // 2, nt // 2

    def tile_a(k):
        j = jnp.minimum(k, n_pairs - 1)
        return (j // pairs_per_seq, j % pairs_per_seq, 0)

    def tile_c(k):
        j = jnp.maximum(k - 1, 0)
        return (j // pairs_per_seq, j % pairs_per_seq, 0)

    return pl.pallas_call(
        functools.partial(_mix_kernel, ts=ts, nt=nt, n_tiles=n_tiles),
        grid=(n_pairs + 1,),
        in_specs=[
            pl.BlockSpec((1, 2 * ts, D_MODEL), tile_a),
            pl.BlockSpec((1, 2 * ts, D_MODEL), tile_c),
            pl.BlockSpec((1, 6, D_MODEL), lambda s: (tile_a(s)[0], 0, 0)),
            pl.BlockSpec((1, 6, D_MODEL), lambda s: (tile_c(s)[0], 0, 0)),
            pl.BlockSpec((1, D_MODEL), const2),
            pl.BlockSpec((D_MODEL, N_MAIN), const2, pipeline_mode=pl.Buffered(1)),
            pl.BlockSpec((BF16_SUBLANES, D_MODEL), const2),
            pl.BlockSpec((SUBLANES, LANES), const2),
            pl.BlockSpec((1, GM_WIDTH), const2),
            pl.BlockSpec((N_HEADS, CHUNK, CHUNK), lambda s: (0, 0, 0)),
            pl.BlockSpec((CHUNK, LANES), const2),
            pl.BlockSpec((ML_CONV, 2 * ML_WIDTH), const2),
            pl.BlockSpec((1, 2 * ML_WIDTH), const2),
            pl.BlockSpec((1, ML_WIDTH), const2),
            pl.BlockSpec((D_MODEL, D_MODEL), const2, pipeline_mode=pl.Buffered(1)),
        ],
        out_specs=pl.BlockSpec((1, 2 * ts, D_MODEL), tile_c),
        out_shape=jax.ShapeDtypeStruct(x.shape, F32),
        scratch_shapes=[
            pltpu.VMEM((ts, D_MODEL), BF16),
            pltpu.VMEM((HIST + ts, N_MAIN), F32),
            pltpu.VMEM((HIST + ts, N_MAIN), F32),
            pltpu.VMEM((4, SUBLANES, ts), F32),
            pltpu.VMEM((4, SUBLANES, ts), F32),
            pltpu.VMEM((ts, D_MODEL), BF16),
            pltpu.VMEM((ts, D_MODEL), BF16),
            pltpu.VMEM((N_HEADS, CHUNK, CHUNK), BF16),
            pltpu.VMEM((N_HEADS, HEAD_DIM, 2 * HEAD_DIM), F32),
            pltpu.VMEM((SUBLANES, LANES), F32),
        ],
        compiler_params=pltpu.CompilerParams(
            dimension_semantics=("arbitrary",),
            vmem_limit_bytes=VMEM_LIMIT_BYTES),
        name="token_mix",
    )(x, x, mod, mod, n1g, w_main, w_gate_t, gate_b, vng, ws, bs_col, cw, cb, hng, w_out)


def _ffn_kernel(x_ref, mod_ref, fmod_ref, n2g_ref, wup_ref, fcw_ref, fcb_ref, wdn_ref, fg_ref, o_ref,
                hb_ref, g_ref, a_ref, *, ts, nb):
    @pl.when(pl.program_id(1) == 0)
    def _():
        g_ref[0:HIST, :] = jnp.zeros((HIST, D_FF), F32)

    mod = mod_ref[0]
    sh2, sc2, g2 = mod[3:4], mod[4:5], mod[5:6]
    fmod = fmod_ref[0]
    fsh, fsc = fmod[0:1], fmod[1:2]
    gain2 = n2g_ref[...] * (1.0 + sc2)
    gainf = fg_ref[...] * (1.0 + fsc)
    th = ts // 2

    def norm(r0):
        for c in range(th // CHUNK):
            rows = slice(r0 + c * CHUNK, r0 + (c + 1) * CHUNK)
            xs = x_ref[0, rows, :]
            hb_ref[rows, :] = (xs * _rms_scale(xs) * gain2 + sh2).astype(BF16)

    def up(r0):
        rows = slice(r0, r0 + th)
        for j in range(D_FF // nb):
            cols = slice(j * nb, (j + 1) * nb)
            g_ref[HIST + r0:HIST + r0 + th, cols] = jnp.dot(hb_ref[rows, :], wup_ref[:, cols],
                                                            preferred_element_type=F32)
            u = jnp.dot(hb_ref[rows, :], wup_ref[:, D_FF + j * nb:D_FF + (j + 1) * nb],
                        preferred_element_type=F32)
            gc = _causal_conv(g_ref, HIST + r0, th, cols, fcw_ref, cols, fcb_ref[:, cols], FFN_CONV)
            a_ref[rows, cols] = (_gelu(gc) * u).astype(BF16)

    def down(r0):
        f = jnp.dot(a_ref[r0:r0 + th, :], wdn_ref[...], preferred_element_type=F32)
        for c in range(th // CHUNK):
            rows = slice(r0 + c * CHUNK, r0 + (c + 1) * CHUNK)
            x2 = x_ref[0, rows, :] + g2 * f[c * CHUNK:(c + 1) * CHUNK, :]
            o_ref[0, rows, :] = x2 * _rms_scale(x2) * gainf + fsh

    norm(0)
    norm(th)
    up(0)
    up(th)
    g_ref[0:HIST, :] = g_ref[ts:ts + HIST, :]
    down(0)
    down(th)


def _channel_mix(x, mod, fmod, n2g, w_up, fcw, fcb, w_down, fg, ts, nb):
    bsz, seq, _ = x.shape
    const2 = lambda b, t: (0, 0)
    resident = pl.Buffered(1)
    return pl.pallas_call(
        functools.partial(_ffn_kernel, ts=ts, nb=nb),
        grid=(bsz, seq // ts),
        in_specs=[
            pl.BlockSpec((1, ts, D_MODEL), lambda b, t: (b, t, 0)),
            pl.BlockSpec((1, 6, D_MODEL), lambda b, t: (b, 0, 0)),
            pl.BlockSpec((1, 2, D_MODEL), lambda b, t: (b, 0, 0)),
            pl.BlockSpec((1, D_MODEL), const2),
            pl.BlockSpec((D_MODEL, 2 * D_FF), const2, pipeline_mode=resident),
            pl.BlockSpec((FFN_CONV, D_FF), const2),
            pl.BlockSpec((1, D_FF), const2),
            pl.BlockSpec((D_FF, D_MODEL), const2, pipeline_mode=resident),
            pl.BlockSpec((1, D_MODEL), const2),
        ],
        out_specs=pl.BlockSpec((1, ts, D_MODEL), lambda b, t: (b, t, 0)),
        out_shape=jax.ShapeDtypeStruct(x.shape, F32),
        scratch_shapes=[
            pltpu.VMEM((ts, D_MODEL), BF16),
            pltpu.VMEM((HIST + ts, D_FF), F32),
            pltpu.VMEM((ts, D_FF), BF16),
        ],
        compiler_params=pltpu.CompilerParams(
            dimension_semantics=("arbitrary", "arbitrary"),
            vmem_limit_bytes=VMEM_LIMIT_BYTES),
        name="channel_mix",
    )(x, mod, fmod, n2g, w_up, fcw, fcb, w_down, fg)


def kernel(x, c, ada_w, ada_b, norm1_g, w_in, gm_vnorm_g, gm_spatial_w, gm_spatial_b, ml_conv_w,
           ml_conv_b, ml_i_b, ml_f_b, ml_hnorm_g, w_out, norm2_g, ffn_w_up, ffn_conv_w, ffn_conv_b,
           ffn_w_down, final_ada_w, final_ada_b, final_g):
    depth = ada_w.shape[0]
    assert depth == 1, "the final norm is fused into the channel-mix call of a single layer"
    bsz = x.shape[0]
    fmod = _modulation(c, final_ada_w, final_ada_b[None, :]).reshape(bsz, 2, D_MODEL)
    for l in range(depth):
        mod = _modulation(c, ada_w[l], ada_b[l][None, :]).reshape(bsz, 6, D_MODEL)

        w_main = w_in[l].astype(BF16)
        w_gate_t = (jnp.zeros((BF16_SUBLANES, D_MODEL), F32).at[0:2 * N_HEADS]
                    .set(w_in[l][:, N_MAIN:].T).astype(BF16))
        gate_b = jnp.broadcast_to(jnp.concatenate([ml_i_b[l], ml_f_b[l]])[:, None], (SUBLANES, LANES))
        bs_col = jnp.zeros((CHUNK, LANES), F32).at[:, 0:N_HEADS].set(gm_spatial_b[l].T)
        x = _token_mix(x, mod, norm1_g[l][None, :], w_main, w_gate_t, gate_b,
                       gm_vnorm_g[l][None, :], gm_spatial_w[l], bs_col, ml_conv_w[l],
                       ml_conv_b[l][None, :], ml_hnorm_g[l][None, :], w_out[l].astype(BF16), ts=TOKEN_TILE)
        x = _channel_mix(x, mod, fmod, norm2_g[l][None, :], ffn_w_up[l].astype(BF16), ffn_conv_w[l],
                         ffn_conv_b[l][None, :], ffn_w_down[l].astype(BF16), final_g[None, :],
                         ts=FFN_TILE, nb=FFN_BLOCK)
    return x
```
